```python
import jax, jax.numpy as jnp
from jax import lax
import numpy as np

D_MODEL = 1024
BATCH = 16
SEQ = 2048
DEPTH = 2

N_MIXERS = 2
N_HGRN_LAYERS = (DEPTH + N_MIXERS - 1) // N_MIXERS
N_CONV_LAYERS = DEPTH // N_MIXERS
MEM_LEN = 256
HGRN_HEADS = 8
HGRN_DIM = D_MODEL // HGRN_HEADS
HGRN_CHUNK = 64
CONV_WIDTH = 31
XATTN_HEADS = 4
XATTN_DIM = D_MODEL // XATTN_HEADS
D_FF = 2816
RMS_EPS = 1e-6
LN_EPS = 1e-5

kernel_name = "hgrn2_conformer_conv_interleaved_macaron_memxattn"


def rms_norm(x, g):
    xf = x.astype(jnp.float32)
    y = xf * lax.rsqrt(jnp.mean(xf * xf, axis=-1, keepdims=True) + RMS_EPS)
    return (y * g.astype(jnp.float32)).astype(x.dtype)


def layer_norm(x, g, b):
    xf = x.astype(jnp.float32)
    mu = jnp.mean(xf, axis=-1, keepdims=True)
    var = jnp.mean(jnp.square(xf - mu), axis=-1, keepdims=True)
    y = (xf - mu) * lax.rsqrt(var + LN_EPS)
    return (y * g.astype(jnp.float32) + b.astype(jnp.float32)).astype(x.dtype)


def swiglu_ffn(h, w_in, w_out):
    gate, up = jnp.split(h @ w_in, 2, axis=-1)
    return (jax.nn.silu(gate) * up) @ w_out


def hgrn2_mixer(h, w_in, head_norm, w_out, lb):
    B, S, D = h.shape
    n_chunks = S // HGRN_CHUNK
    f32 = jnp.float32
    q, f, i, g = jnp.split(h @ w_in, 4, axis=-1)
    q = jax.nn.silu(q.astype(f32))
    ff = f.astype(f32)
    lbf = lb.astype(f32)
    log_forget = jnp.log(lbf + (1.0 - lbf) * jax.nn.sigmoid(ff))
    k = (1.0 - lbf) * jax.nn.sigmoid(-ff)
    v = i.astype(f32)

    def to_chunks(t):
        return t.reshape(B, n_chunks, HGRN_CHUNK, HGRN_HEADS, HGRN_DIM).transpose(1, 0, 3, 2, 4)

    causal = jnp.tril(jnp.ones((HGRN_CHUNK, HGRN_CHUNK), dtype=bool))

    def chunk_step(state, inp):
        qb, kb, vb, gb = inp
        G = jnp.cumsum(gb, axis=2)
        o_inter = jnp.einsum('bhtk,bhkv->bhtv', qb * jnp.exp(G), state)
        diff = G[:, :, :, None, :] - G[:, :, None, :, :]
        decay = jnp.exp(jnp.where(causal[None, None, :, :, None], diff, -jnp.inf))
        scores = jnp.einsum('bhtk,bhsk,bhtsk->bhts', qb, kb, decay)
        o_intra = jnp.einsum('bhts,bhsv->bhtv', scores, vb)
        G_last = G[:, :, -1:, :]
        k_dec = kb * jnp.exp(G_last - G)
        new_state = state * jnp.exp(G_last[:, :, 0, :, None]) + jnp.einsum('bhsk,bhsv->bhkv', k_dec, vb)
        return new_state, o_inter + o_intra

    state0 = jnp.zeros((B, HGRN_HEADS, HGRN_DIM, HGRN_DIM), f32)
    _, o = lax.scan(chunk_step, state0, (to_chunks(q), to_chunks(k), to_chunks(v), to_chunks(log_forget)))
    o = o.transpose(1, 0, 3, 2, 4).reshape(B, S, HGRN_HEADS, HGRN_DIM).astype(h.dtype)
    o = rms_norm(o, head_norm.reshape(HGRN_HEADS, HGRN_DIM)).reshape(B, S, D)
    o = o * jax.nn.silu(g)
    return o @ w_out


def conformer_conv_mixer(h, w_in, b_in, dw, dw_b, ln_g, ln_b, w_out, b_out):
    D = h.shape[-1]
    a, gate = jnp.split(h @ w_in + b_in, 2, axis=-1)
    u = a * jax.nn.sigmoid(gate)
    u = lax.conv_general_dilated(
        u, dw.astype(u.dtype)[:, None, :], window_strides=(1,),
        padding=[(CONV_WIDTH - 1, 0)],
        dimension_numbers=('NWC', 'WIO', 'NWC'), feature_group_count=D) + dw_b
    u = jax.nn.silu(layer_norm(u, ln_g, ln_b))
    return u @ w_out + b_out


def memory_cross_attention(h, mem_n, wq, wkv, wo):
    B, S, D = h.shape
    q = (h @ wq).reshape(B, S, XATTN_HEADS, XATTN_DIM)
    k, v = jnp.split(mem_n @ wkv, 2, axis=-1)
    k = k.reshape(B, MEM_LEN, XATTN_HEADS, XATTN_DIM)
    v = v.reshape(B, MEM_LEN, XATTN_HEADS, XATTN_DIM)
    s = jnp.einsum('bshd,bmhd->bhsm', q, k).astype(jnp.float32) * (XATTN_DIM ** -0.5)
    p = jax.nn.softmax(s, axis=-1).astype(v.dtype)
    o = jnp.einsum('bhsm,bmhd->bshd', p, v).reshape(B, S, D)
    return o @ wo


def setup_inputs(seed: int = 0) -> dict:
    key = jax.random.key(seed)
    ks = iter(jax.random.split(key, 40))
    D, F, L = D_MODEL, D_FF, DEPTH
    nA, nB = N_HGRN_LAYERS, N_CONV_LAYERS

    def w(shape, fan_in):
        return jax.random.normal(next(ks), shape, jnp.float32) * (fan_in ** -0.5)

    def gain(shape):
        return 1.0 + 0.02 * jax.random.normal(next(ks), shape, jnp.float32)

    def bias(shape):
        return 0.02 * jax.random.normal(next(ks), shape, jnp.float32)

    return {
        "x": jax.random.normal(next(ks), (BATCH, SEQ, D), jnp.float32),
        "mem": jax.random.normal(next(ks), (BATCH, MEM_LEN, D), jnp.float32),
        "ffn1_norm": gain((L, D)),
        "ffn1_w_in": w((L, D, 2 * F), D),
        "ffn1_w_out": w((L, F, D), F),
        "mix_norm": gain((L, D)),
        "hgrn_w_in": w((nA, D, 4 * D), D),
        "hgrn_head_norm": gain((nA, D)),
        "hgrn_w_out": w((nA, D, D), D),
        "hgrn_lb_logits": 0.1 * jax.random.normal(next(ks), (nA + 1, D), jnp.float32),
        "conv_w_in": w((nB, D, 2 * D), D),
        "conv_b_in": bias((nB, 2 * D)),
        "conv_dw": w((nB, CONV_WIDTH, D), CONV_WIDTH),
        "conv_dw_b": bias((nB, D)),
        "conv_ln_g": gain((nB, D)),
        "conv_ln_b": bias((nB, D)),
        "conv_w_out": w((nB, D, D), D),
        "conv_b_out": bias((nB, D)),
        "xattn_norm": gain((L, D)),
        "xattn_wq": w((L, D, D), D),
        "xattn_wkv": w((L, D, 2 * D), D),
        "xattn_wo": w((L, D, D), D),
        "ffn2_norm": gain((L, D)),
        "ffn2_w_in": w((L, D, 2 * F), D),
        "ffn2_w_out": w((L, F, D), F),
        "mem_norm": gain((D,)),
        "final_norm": gain((D,)),
    }


def reference(x, mem, ffn1_norm, ffn1_w_in, ffn1_w_out, mix_norm,
              hgrn_w_in, hgrn_head_norm, hgrn_w_out, hgrn_lb_logits,
              conv_w_in, conv_b_in, conv_dw, conv_dw_b, conv_ln_g, conv_ln_b, conv_w_out, conv_b_out,
              xattn_norm, xattn_wq, xattn_wkv, xattn_wo,
              ffn2_norm, ffn2_w_in, ffn2_w_out, mem_norm, final_norm):
    mem_n = rms_norm(mem, mem_norm)
    lower_bounds = jnp.cumsum(jax.nn.softmax(hgrn_lb_logits.astype(jnp.float32), axis=0), axis=0)
    h = x
    for layer in range(DEPTH):
        h = h + 0.5 * swiglu_ffn(rms_norm(h, ffn1_norm[layer]), ffn1_w_in[layer], ffn1_w_out[layer])
        hn = rms_norm(h, mix_norm[layer])
        j = layer // N_MIXERS
        if layer % N_MIXERS == 0:
            mixed = hgrn2_mixer(hn, hgrn_w_in[j], hgrn_head_norm[j], hgrn_w_out[j], lower_bounds[j])
        else:
            mixed = conformer_conv_mixer(hn, conv_w_in[j], conv_b_in[j], conv_dw[j], conv_dw_b[j],
                                         conv_ln_g[j], conv_ln_b[j], conv_w_out[j], conv_b_out[j])
        h = h + mixed.astype(h.dtype)
        h = h + memory_cross_attention(rms_norm(h, xattn_norm[layer]), mem_n,
                                       xattn_wq[layer], xattn_wkv[layer], xattn_wo[layer])
        h = h + 0.5 * swiglu_ffn(rms_norm(h, ffn2_norm[layer]), ffn2_w_in[layer], ffn2_w_out[layer])
    return rms_norm(h, final_norm)
```

```python
import functools

import numpy as np
import jax
import jax.numpy as jnp
from jax import lax
from jax.experimental import pallas as pl
from jax.experimental.pallas import tpu as pltpu

F32 = jnp.float32
BF16 = jnp.bfloat16

RMS_EPS = 1e-6
LN_EPS = 1e-5

HGRN_HEAD_DIM = 128
XATTN_HEADS = 4
CONV_WIDTH = 31

LANES = 128
SUBLANES = 8
VMEM_LIMIT_BYTES = 56 * 1024 * 1024

HGRN_TILE = 256
HGRN_CHUNK = 64
HGRN_SUB = 16

FFN_ROWS = 512
FFN_COLS = 256
ROW_TILE = 512
CONV_ROWS = 256
CONV_HALO = 32
CONV_STRIP = 8


def _const_spec(shape):
    zeros = (0,) * len(shape)
    return pl.BlockSpec(shape, lambda *_: zeros, pipeline_mode=pl.Buffered(1))


def _params(*semantics):
    return pltpu.CompilerParams(dimension_semantics=semantics,
                                vmem_limit_bytes=VMEM_LIMIT_BYTES)


def _rms(x, g):
    ms = jnp.mean(x * x, axis=-1, keepdims=True)
    return x * lax.rsqrt(ms + RMS_EPS) * g


def _dot(a, b):
    return jnp.dot(a, b, preferred_element_type=F32)


def _dot_nt(a, b):
    return lax.dot_general(a, b, (((1,), (1,)), ((), ())), preferred_element_type=F32)


def _dot_tn(a, b):
    return lax.dot_general(a, b, (((0,), (0,)), ((), ())), preferred_element_type=F32)


def _ffn_kernel(*refs, d_ff, final):
    if final:
        x_ref, g_ref, win_ref, wout_ref, fg_ref, o_ref, act_ref = refs
    else:
        x_ref, g_ref, win_ref, wout_ref, o_ref, act_ref = refs
    x = x_ref[...]
    xn = _rms(x, g_ref[...]).astype(BF16)
    for c in range(d_ff // FFN_COLS):
        lo = c * FFN_COLS
        gate = _dot(xn, win_ref[:, lo:lo + FFN_COLS])
        up = _dot(xn, win_ref[:, d_ff + lo:d_ff + lo + FFN_COLS])
        act_ref[:, lo:lo + FFN_COLS] = (gate * jax.nn.sigmoid(gate) * up).astype(BF16)
    y = x + 0.5 * _dot(act_ref[...], wout_ref[...])
    if final:
        y = _rms(y, fg_ref[...])
    o_ref[...] = y


def _ffn(h, norm_g, w_in, w_out, final_g=None):
    t, d = h.shape
    d_ff = w_out.shape[0]
    assert t % FFN_ROWS == 0 and d_ff % FFN_COLS == 0
    final = final_g is not None
    row_spec = pl.BlockSpec((FFN_ROWS, d), lambda i: (i, 0))
    in_specs = [row_spec, _const_spec((1, d)), _const_spec((d, 2 * d_ff)), _const_spec((d_ff, d))]
    args = [h, norm_g.reshape(1, d), w_in, w_out]
    if final:
        in_specs.append(_const_spec((1, d)))
        args.append(final_g.reshape(1, d))
    return pl.pallas_call(
        functools.partial(_ffn_kernel, d_ff=d_ff, final=final),
        grid=(t // FFN_ROWS,),
        in_specs=in_specs,
        out_specs=row_spec,
        out_shape=jax.ShapeDtypeStruct((t, d), F32),
        scratch_shapes=[pltpu.VMEM((FFN_ROWS, d_ff), BF16)],
        compiler_params=_params("parallel"),
        name="ffn",
    )(*args)


def _hgrn_proj_kernel(x_ref, g_ref, w_ref, lbl_ref, q_ref, k_ref, v_ref, lf_ref, og_ref, *, layer_slot):
    x = x_ref[...]
    d = x.shape[1]
    xn = _rms(x, g_ref[...]).astype(BF16)
    logits = lbl_ref[...]
    e = jnp.exp(logits - jnp.max(logits, axis=0, keepdims=True))
    sm = e / jnp.sum(e, axis=0, keepdims=True)
    lb = jnp.sum(sm[0:layer_slot + 1], axis=0, keepdims=True)
    q = _dot(xn, w_ref[:, 0:d])
    q_ref[...] = q * jax.nn.sigmoid(q)
    f = _dot(xn, w_ref[:, d:2 * d])
    lf_ref[...] = jnp.log(lb + (1.0 - lb) * jax.nn.sigmoid(f))
    k_ref[...] = (1.0 - lb) * jax.nn.sigmoid(-f)
    v_ref[...] = _dot(xn, w_ref[:, 2 * d:3 * d])
    og = _dot(xn, w_ref[:, 3 * d:4 * d])
    og_ref[...] = og * jax.nn.sigmoid(og)


def _hgrn_proj(h, norm_g, w_in, lb_logits, layer_slot):
    t, d = h.shape
    n_slots = lb_logits.shape[0]
    row_spec = pl.BlockSpec((ROW_TILE, d), lambda i: (i, 0))
    out = jax.ShapeDtypeStruct((t, d), F32)
    return pl.pallas_call(
        functools.partial(_hgrn_proj_kernel, layer_slot=layer_slot),
        grid=(t // ROW_TILE,),
        in_specs=[row_spec, _const_spec((1, d)), _const_spec((d, 4 * d)), _const_spec((n_slots, d))],
        out_specs=[row_spec] * 5,
        out_shape=[out] * 5,
        compiler_params=_params("parallel"),
        name="hgrn_proj",
    )(h, norm_g.reshape(1, d), w_in, lb_logits)


def _hgrn_tables():
    per_lanes = 2 * LANES // HGRN_SUB
    sel = np.zeros((HGRN_SUB * LANES, 2 * LANES), np.float32)
    for s in range(HGRN_SUB):
        sel[s * LANES:(s + 1) * LANES, s::HGRN_SUB] = 1.0
    assert per_lanes * HGRN_SUB == 2 * LANES
    r = np.arange(HGRN_TILE)
    same_chunk = (r[:, None] // HGRN_CHUNK) == (r[None, :] // HGRN_CHUNK)
    dist = r[:, None] // HGRN_SUB - r[None, :] // HGRN_SUB
    dist = np.where(same_chunk & (dist >= 0), dist, -1).astype(np.int32)
    return jnp.asarray(sel, BF16), jnp.asarray(dist)


def _hgrn_rec_kernel(q_ref, k_ref, v_ref, lf_ref, sel_ref, dist_ref, o_ref, st_ref):
    ts, dh = q_ref.shape
    sub, chunk = HGRN_SUB, HGRN_CHUNK
    nb, nc, per = ts // sub, ts // chunk, chunk // sub
    half = sub // 2

    @pl.when(pl.program_id(2) == 0)
    def _():
        st_ref[...] = jnp.zeros_like(st_ref)

    q = q_ref[...]
    k = k_ref[...]
    v = v_ref[...]

    row = lax.broadcasted_iota(jnp.int32, (ts, dh), 0) % chunk
    G = lf_ref[...]
    sh = 1
    while sh < chunk:
        G = G + jnp.where(row >= sh, pltpu.roll(G, sh, axis=0), 0.0)
        sh *= 2

    q3 = q.reshape(nb, sub, dh)
    k3 = k.reshape(nb, sub, dh)
    G3 = G.reshape(nb, sub, dh)
    last = G3[:, sub - 1:sub, :]
    blk = lax.broadcasted_iota(jnp.int32, (nb, 1, dh), 0) % per

    k_side = (k3 * jnp.exp(last - G3)).reshape(ts, dh).astype(BF16)
    q_side = []
    for delta in range(1, per):
        ref_g = jnp.concatenate([jnp.zeros((delta, 1, dh), F32), last[:nb - delta]], axis=0)
        expo = jnp.where(blk >= delta, G3 - ref_g, 0.0)
        q_side.append((q3 * jnp.exp(expo)).reshape(ts, dh))
    cross = _dot_nt(jnp.concatenate(q_side, axis=0).astype(BF16), k_side)

    sub_iota = lax.broadcasted_iota(jnp.int32, (nb, half, dh), 1)
    q_top, q_bot = q3[:, :half], q3[:, half:]
    g_top, g_bot = G3[:, :half], G3[:, half:]
    p_top, p_bot = [], []
    for s in range(sub):
        g_s = G3[:, s:s + 1, :]
        k_s = k3[:, s:s + 1, :]
        if s < half:
            d_top = jnp.where(sub_iota >= s, g_top - g_s, -jnp.inf)
            p_top.append(jnp.exp(d_top) * q_top * k_s)
            p_bot.append(jnp.exp(g_bot - g_s) * q_bot * k_s)
        else:
            d_bot = jnp.where(sub_iota >= s - half, g_bot - g_s, -jnp.inf)
            p_bot.append(jnp.exp(d_bot) * q_bot * k_s)
    p_top = jnp.concatenate(p_top, axis=-1).reshape(nb * half, half * dh).astype(BF16)
    p_bot = jnp.concatenate(p_bot, axis=-1).reshape(nb * half, sub * dh).astype(BF16)
    a_top = _dot(p_top, sel_ref[0:half * dh, :]).reshape(nb, half, ts)
    a_bot = _dot(p_bot, sel_ref[...]).reshape(nb, half, ts)
    within = jnp.concatenate([a_top, a_bot], axis=1).reshape(ts, ts)

    dist = dist_ref[...]
    scores = jnp.where(dist == 0, within, 0.0)
    for delta in range(1, per):
        scores = jnp.where(dist == delta, cross[(delta - 1) * ts:delta * ts], scores)
    vb = v.astype(BF16)
    o_intra = _dot(scores.astype(BF16), vb)

    G4 = G.reshape(nc, chunk, dh)
    g_end = G4[:, chunk - 1:chunk, :]
    q_dec = (q * jnp.exp(G)).astype(BF16)
    k_dec = (k.reshape(nc, chunk, dh) * jnp.exp(g_end - G4)).astype(BF16)
    state = st_ref[...]
    outs = []
    for c in range(nc):
        rows = slice(c * chunk, (c + 1) * chunk)
        outs.append(_dot_nt(q_dec[rows], state.astype(BF16)))
        state = state * jnp.exp(g_end[c]) + _dot_tn(vb[rows], k_dec[c])
    st_ref[...] = state
    o_ref[...] = jnp.concatenate(outs, axis=0) + o_intra


def _hgrn_rec(q, k, v, lf):
    b, s, d = q.shape
    dh = HGRN_HEAD_DIM
    assert s % HGRN_TILE == 0 and d % dh == 0
    sel, dist = _hgrn_tables()
    head_spec = pl.BlockSpec((None, HGRN_TILE, dh), lambda bi, hi, si: (bi, si, hi))
    return pl.pallas_call(
        _hgrn_rec_kernel,
        grid=(b, d // dh, s // HGRN_TILE),
        in_specs=[head_spec] * 4 + [_const_spec(sel.shape), _const_spec(dist.shape)],
        out_specs=head_spec,
        out_shape=jax.ShapeDtypeStruct((b, s, d), F32),
        scratch_shapes=[pltpu.VMEM((dh, dh), F32)],
        compiler_params=_params("parallel", "parallel", "arbitrary"),
        name="hgrn_rec",
    )(q, k, v, lf, sel, dist)


def _hgrn_out_kernel(x_ref, o_ref, og_ref, hn_ref, w_ref, y_ref, act_ref):
    d = x_ref.shape[1]
    for h in range(d // HGRN_HEAD_DIM):
        cols = slice(h * HGRN_HEAD_DIM, (h + 1) * HGRN_HEAD_DIM)
        act_ref[:, cols] = (_rms(o_ref[:, cols], hn_ref[:, cols]) * og_ref[:, cols]).astype(BF16)
    y_ref[...] = x_ref[...] + _dot(act_ref[...], w_ref[...])


def _hgrn_out(h, o, og, head_norm, w_out):
    t, d = h.shape
    row_spec = pl.BlockSpec((ROW_TILE, d), lambda i: (i, 0))
    return pl.pallas_call(
        _hgrn_out_kernel,
        grid=(t // ROW_TILE,),
        in_specs=[row_spec, row_spec, row_spec, _const_spec((1, d)), _const_spec((d, d))],
        out_specs=row_spec,
        out_shape=jax.ShapeDtypeStruct((t, d), F32),
        scratch_shapes=[pltpu.VMEM((ROW_TILE, d), BF16)],
        compiler_params=_params("parallel"),
        name="hgrn_out",
    )(h, o, og, head_norm.reshape(1, d), w_out)


def _conv_kernel(x_ref, g_ref, win_ref, bin_ref, dw_ref, dwb_ref, lng_ref, lnb_ref, wout_ref, bout_ref,
                 y_ref, u_ref, c_ref):
    ts, d = x_ref.shape
    halo, strip = CONV_HALO, CONV_STRIP

    @pl.when(pl.program_id(1) == 0)
    def _():
        u_ref[0:halo, :] = jnp.zeros((halo, d), F32)

    x = x_ref[...]
    xn = _rms(x, g_ref[...]).astype(BF16)
    a = _dot(xn, win_ref[:, 0:d]) + bin_ref[:, 0:d]
    gate = _dot(xn, win_ref[:, d:2 * d]) + bin_ref[:, d:2 * d]
    u_ref[halo:halo + ts, :] = a * jax.nn.sigmoid(gate)

    first = halo - (CONV_WIDTH - 1)

    for base in range(0, ts, strip):
        acc = jnp.broadcast_to(dwb_ref[...], (strip, d))
        for j in range(CONV_WIDTH):
            lo = base + first + j
            acc = acc + dw_ref[j:j + 1, :] * u_ref[lo:lo + strip, :]
        c_ref[base:base + strip, :] = acc
    u_ref[0:halo, :] = u_ref[ts:ts + halo, :]

    c = c_ref[...]
    mu = jnp.mean(c, axis=-1, keepdims=True)
    cc = c - mu
    var = jnp.mean(cc * cc, axis=-1, keepdims=True)
    z = cc * lax.rsqrt(var + LN_EPS) * lng_ref[...] + lnb_ref[...]
    z = (z * jax.nn.sigmoid(z)).astype(BF16)
    y_ref[...] = x + _dot(z, wout_ref[...]) + bout_ref[...]


def _conv_mixer(h, norm_g, w_in, b_in, dw, dw_b, ln_g, ln_b, w_out, b_out):
    b, s, d = h.shape
    ts = CONV_ROWS
    assert s % ts == 0 and ts % CONV_STRIP == 0 and CONV_HALO >= CONV_WIDTH - 1
    tile_spec = pl.BlockSpec((None, ts, d), lambda bi, si: (bi, si, 0))
    vec = lambda a: a.reshape(1, -1)
    return pl.pallas_call(
        _conv_kernel,
        grid=(b, s // ts),
        in_specs=[tile_spec, _const_spec((1, d)), _const_spec((d, 2 * d)), _const_spec((1, 2 * d)),
                  _const_spec((CONV_WIDTH, d)), _const_spec((1, d)), _const_spec((1, d)),
                  _const_spec((1, d)), _const_spec((d, d)), _const_spec((1, d))],
        out_specs=tile_spec,
        out_shape=jax.ShapeDtypeStruct((b, s, d), F32),
        scratch_shapes=[pltpu.VMEM((ts + CONV_HALO, d), F32), pltpu.VMEM((ts, d), F32)],
        compiler_params=_params("parallel", "arbitrary"),
        name="conv_mixer",
    )(h, vec(norm_g), w_in, vec(b_in), dw, vec(dw_b), vec(ln_g), vec(ln_b), w_out, vec(b_out))


def _memkv_kernel(m_ref, g_ref, w_ref, kv_ref):
    mn = _rms(m_ref[...], g_ref[...]).astype(BF16)
    kv_ref[...] = _dot(mn, w_ref[...]).astype(BF16)


def _memkv(mem2d, mem_norm, wkv):
    n, d = mem2d.shape
    n_layers = wkv.shape[0]
    rows = min(ROW_TILE, n)
    return pl.pallas_call(
        _memkv_kernel,
        grid=(n_layers, n // rows),
        in_specs=[pl.BlockSpec((rows, d), lambda li, i: (i, 0)), _const_spec((1, d)),
                  pl.BlockSpec((None, d, 2 * d), lambda li, i: (li, 0, 0))],
        out_specs=pl.BlockSpec((None, rows, 2 * d), lambda li, i: (li, i, 0)),
        out_shape=jax.ShapeDtypeStruct((n_layers, n, 2 * d), BF16),
        compiler_params=_params("arbitrary", "arbitrary"),
        name="mem_kv",
    )(mem2d, mem_norm.reshape(1, d), wkv)


def _xattn_kernel(x_ref, g_ref, wq_ref, kv_ref, wo_ref, y_ref, act_ref, *, heads):
    ts, d = x_ref.shape
    dh = d // heads
    scale = dh ** -0.5
    x = x_ref[...]
    xn = _rms(x, g_ref[...]).astype(BF16)
    q = _dot(xn, wq_ref[...]).astype(BF16)
    for h in range(heads):
        cols = slice(h * dh, (h + 1) * dh)
        s = _dot_nt(q[:, cols], kv_ref[:, cols]) * scale
        s = s - jnp.max(s, axis=-1, keepdims=True)
        p = jnp.exp(s)
        p = p / jnp.sum(p, axis=-1, keepdims=True)
        act_ref[:, cols] = _dot(p.astype(BF16), kv_ref[:, d + h * dh:d + (h + 1) * dh]).astype(BF16)
    y_ref[...] = x + _dot(act_ref[...], wo_ref[...])


def _xattn(h, norm_g, wq, kv, wo, heads):
    b, s, d = h.shape
    m = kv.shape[1]
    ts = ROW_TILE
    tile_spec = pl.BlockSpec((None, ts, d), lambda bi, si: (bi, si, 0))
    return pl.pallas_call(
        functools.partial(_xattn_kernel, heads=heads),
        grid=(b, s // ts),
        in_specs=[tile_spec, _const_spec((1, d)), _const_spec((d, d)),
                  pl.BlockSpec((None, m, 2 * d), lambda bi, si: (bi, 0, 0)), _const_spec((d, d))],
        out_specs=tile_spec,
        out_shape=jax.ShapeDtypeStruct((b, s, d), F32),
        scratch_shapes=[pltpu.VMEM((ts, d), BF16)],
        compiler_params=_params("parallel", "parallel"),
        name="xattn",
    )(h, norm_g.reshape(1, d), wq, kv, wo)


def kernel(x, mem, ffn1_norm, ffn1_w_in, ffn1_w_out, mix_norm, hgrn_w_in, hgrn_head_norm, hgrn_w_out,
           hgrn_lb_logits, conv_w_in, conv_b_in, conv_dw, conv_dw_b, conv_ln_g, conv_ln_b, conv_w_out,
           conv_b_out, xattn_norm, xattn_wq, xattn_wkv, xattn_wo, ffn2_norm, ffn2_w_in, ffn2_w_out,
           mem_norm, final_norm):
    b, s, d = x.shape
    m = mem.shape[1]
    depth = ffn1_norm.shape[0]
    t = b * s
    bf = lambda w: w.astype(BF16)

    kv = _memkv(mem.reshape(b * m, d), mem_norm, bf(xattn_wkv)).reshape(depth, b, m, 2 * d)
    h = x.reshape(t, d)
    for layer in range(depth):
        h = _ffn(h, ffn1_norm[layer], bf(ffn1_w_in[layer]), bf(ffn1_w_out[layer]))
        j = layer // 2
        if layer % 2 == 0:
            q, k, v, lf, og = _hgrn_proj(h, mix_norm[layer], bf(hgrn_w_in[j]), hgrn_lb_logits, j)
            to3 = lambda a: a.reshape(b, s, d)
            o = _hgrn_rec(to3(q), to3(k), to3(v), to3(lf)).reshape(t, d)
            h = _hgrn_out(h, o, og, hgrn_head_norm[j], bf(hgrn_w_out[j]))
        else:
            h = _conv_mixer(h.reshape(b, s, d), mix_norm[layer], bf(conv_w_in[j]), conv_b_in[j],
                            conv_dw[j], conv_dw_b[j], conv_ln_g[j], conv_ln_b[j], bf(conv_w_out[j]),
                            conv_b_out[j]).reshape(t, d)
        h = _xattn(h.reshape(b, s, d), xattn_norm[layer], bf(xattn_wq[layer]), kv[layer],
                   bf(xattn_wo[layer]), XATTN_HEADS).reshape(t, d)
        last = layer == depth - 1
        h = _ffn(h, ffn2_norm[layer], bf(ffn2_w_in[layer]), bf(ffn2_w_out[layer]),
                 final_norm if last else None)
    return h.reshape(b, s, d)
```

```python
import functools

import numpy as np
import jax
import jax.numpy as jnp
from jax import lax
from jax.experimental import pallas as pl
from jax.experimental.pallas import tpu as pltpu

F32 = jnp.float32
BF16 = jnp.bfloat16

RMS_EPS = 1e-6
LN_EPS = 1e-5

HGRN_HEAD_DIM = 128
XATTN_HEADS = 4
CONV_WIDTH = 31

LANES = 128
SUBLANES = 8
VMEM_LIMIT_BYTES = 56 * 1024 * 1024

HGRN_TILE = 2 * LANES
HGRN_CHUNK = 64
HGRN_SUB = 2 * SUBLANES

FFN_ROWS = 512
FFN_COLS = 256
ROW_TILE = 512
XATTN_ROWS = 1024
CONV_ROWS = 256
CONV_HALO = -(-(CONV_WIDTH - 1) // SUBLANES) * SUBLANES


def _const_spec(shape):
    zeros = (0,) * len(shape)
    return pl.BlockSpec(shape, lambda *_: zeros, pipeline_mode=pl.Buffered(1))


def _layer_spec(shape, layer):
    zeros = (0,) * len(shape)
    return pl.BlockSpec((None,) + tuple(shape), lambda *_: (layer,) + zeros,
                        pipeline_mode=pl.Buffered(1))


def _params(*semantics):
    return pltpu.CompilerParams(dimension_semantics=semantics,
                                vmem_limit_bytes=VMEM_LIMIT_BYTES)


def _rms(x, g):
    ms = jnp.mean(x * x, axis=-1, keepdims=True)
    return x * lax.rsqrt(ms + RMS_EPS) * g


def _dot(a, b):
    return jnp.dot(a, b, preferred_element_type=F32)


def _dot_nt(a, b):
    return lax.dot_general(a, b, (((1,), (1,)), ((), ())), preferred_element_type=F32)


def _dot_tn(a, b):
    return lax.dot_general(a, b, (((0,), (0,)), ((), ())), preferred_element_type=F32)


def _ffn_kernel(*refs, d_ff, final):
    if final:
        x_ref, g_ref, win_ref, wout_ref, fg_ref, o_ref, act_ref = refs
    else:
        x_ref, g_ref, win_ref, wout_ref, o_ref, act_ref = refs
    x = x_ref[...]
    xn = _rms(x, g_ref[...]).astype(BF16)
    for c in range(d_ff // FFN_COLS):
        lo = c * FFN_COLS
        gate = _dot(xn, win_ref[:, lo:lo + FFN_COLS])
        up = _dot(xn, win_ref[:, d_ff + lo:d_ff + lo + FFN_COLS])
        act_ref[:, lo:lo + FFN_COLS] = (gate * jax.nn.sigmoid(gate) * up).astype(BF16)
    y = x + 0.5 * _dot(act_ref[...], wout_ref[...])
    if final:
        y = _rms(y, fg_ref[...])
    o_ref[...] = y


def _ffn(h, norm_g, w_in, w_out, layer, final_g=None):
    t, d = h.shape
    d_ff = w_out.shape[1]
    assert t % FFN_ROWS == 0 and d_ff % FFN_COLS == 0
    final = final_g is not None
    row_spec = pl.BlockSpec((FFN_ROWS, d), lambda i: (i, 0))
    in_specs = [row_spec, _layer_spec((1, d), layer), _layer_spec((d, 2 * d_ff), layer),
                _layer_spec((d_ff, d), layer)]
    args = [h, norm_g, w_in, w_out]
    if final:
        in_specs.append(_const_spec((1, d)))
        args.append(final_g.reshape(1, d))
    return pl.pallas_call(
        functools.partial(_ffn_kernel, d_ff=d_ff, final=final),
        grid=(t // FFN_ROWS,),
        in_specs=in_specs,
        out_specs=row_spec,
        out_shape=jax.ShapeDtypeStruct((t, d), F32),
        scratch_shapes=[pltpu.VMEM((FFN_ROWS, d_ff), BF16)],
        compiler_params=_params("parallel"),
        name="ffn",
    )(*args)


def _hgrn_tables():
    sel = np.zeros((HGRN_SUB * LANES, HGRN_TILE), np.float32)
    for s in range(HGRN_SUB):
        sel[s * LANES:(s + 1) * LANES, s::HGRN_SUB] = 1.0
    r = np.arange(HGRN_TILE)
    same_chunk = (r[:, None] // HGRN_CHUNK) == (r[None, :] // HGRN_CHUNK)
    dist = r[:, None] // HGRN_SUB - r[None, :] // HGRN_SUB
    dist = np.where(same_chunk & (dist >= 0), dist, -1).astype(np.int32)
    return jnp.asarray(sel, BF16), jnp.asarray(dist)


def _hgrn_kernel(x_ref, g_ref, win_ref, lbl_ref, hn_ref, wout_ref, sel_ref, dist_ref, y_ref,
                 q_ref, k_ref, v_ref, cum_ref, og_ref, o_ref, ptop_ref, pbot_ref, act_ref, st_ref, *, layer_slot):
    ts, d = x_ref.shape
    dh, sub, chunk = HGRN_HEAD_DIM, HGRN_SUB, HGRN_CHUNK
    heads, nb, nc, per = d // dh, ts // sub, ts // chunk, chunk // sub
    half = sub // 2
    head_cols = [slice(h * dh, (h + 1) * dh) for h in range(heads)]

    @pl.when(pl.program_id(1) == 0)
    def _():
        st_ref[...] = jnp.zeros_like(st_ref)

    x = x_ref[...]
    xn = _rms(x, g_ref[...]).astype(BF16)
    logits = lbl_ref[...]
    e = jnp.exp(logits - jnp.max(logits, axis=0, keepdims=True))
    sm = e / jnp.sum(e, axis=0, keepdims=True)
    lb = jnp.sum(sm[0:layer_slot + 1], axis=0, keepdims=True)

    qp = _dot(xn, win_ref[:, 0:d])
    q_ref[...] = qp * jax.nn.sigmoid(qp)
    fp = _dot(xn, win_ref[:, d:2 * d])
    k_ref[...] = (1.0 - lb) * jax.nn.sigmoid(-fp)
    cum = jnp.log2(lb + (1.0 - lb) * jax.nn.sigmoid(fp))
    row = lax.broadcasted_iota(jnp.int32, (ts, d), 0) % chunk
    sh = 1
    while sh < chunk:
        cum = cum + jnp.where(row >= sh, pltpu.roll(cum, sh, axis=0), 0.0)
        sh *= 2
    cum_ref[...] = cum
    v_ref[...] = _dot(xn, win_ref[:, 2 * d:3 * d]).astype(BF16)
    gp = _dot(xn, win_ref[:, 3 * d:4 * d])
    og_ref[...] = gp * jax.nn.sigmoid(gp)

    q_dec, k_dec, c_end = [], [], []
    for cols in head_cols:
        cum_h = cum_ref[:, cols]
        c4 = cum_h.reshape(nc, chunk, dh)
        end = c4[:, chunk - 1:chunk, :]
        c_end.append(end)
        q_dec.append((q_ref[:, cols] * jnp.exp2(cum_h)).astype(BF16))
        k_dec.append((k_ref[:, cols].reshape(nc, chunk, dh) * jnp.exp2(end - c4)).astype(BF16))
    states = [st_ref[h] for h in range(heads)]
    for c in range(nc):
        crow = slice(c * chunk, (c + 1) * chunk)
        for h, cols in enumerate(head_cols):
            o_ref[crow, cols] = _dot_nt(q_dec[h][crow], states[h].astype(BF16))
            states[h] = states[h] * jnp.exp2(c_end[h][c]) + _dot_tn(v_ref[crow, cols], k_dec[h][c])
    for h in range(heads):
        st_ref[h] = states[h]

    dist = dist_ref[...]
    dist_is = [dist == delta for delta in range(per)]
    blk = lax.broadcasted_iota(jnp.int32, (nb, 1, dh), 0) % per
    sub_iota = lax.broadcasted_iota(jnp.int32, (half, dh), 0)
    group = 2
    for h0 in range(0, heads, group):
        for h in range(h0, h0 + group):
            cols = head_cols[h]
            for b0 in range(0, nb, 2):
                tops, bots = [], []
                for r0 in (b0 * sub, (b0 + 1) * sub):
                    q_top, q_bot = q_ref[r0:r0 + half, cols], q_ref[r0 + half:r0 + sub, cols]
                    c_top, c_bot = cum_ref[r0:r0 + half, cols], cum_ref[r0 + half:r0 + sub, cols]
                    p_top, p_bot = [], []
                    for s in range(sub):
                        c_s = cum_ref[r0 + s:r0 + s + 1, cols]
                        k_s = k_ref[r0 + s:r0 + s + 1, cols]
                        if s < half:
                            d_top = c_top - c_s
                            if s > 0:
                                d_top = jnp.where(sub_iota >= s, d_top, -jnp.inf)
                            p_top.append(jnp.exp2(d_top) * q_top * k_s)
                            p_bot.append(jnp.exp2(c_bot - c_s) * q_bot * k_s)
                        else:
                            d_bot = c_bot - c_s
                            if s > half:
                                d_bot = jnp.where(sub_iota >= s - half, d_bot, -jnp.inf)
                            p_bot.append(jnp.exp2(d_bot) * q_bot * k_s)
                    tops.append(jnp.concatenate(p_top, axis=-1))
                    bots.append(jnp.concatenate(p_bot, axis=-1))
                prow = (h * nb + b0) * half
                ptop_ref[prow:prow + 2 * half, :] = jnp.concatenate(tops, axis=0).astype(BF16)
                pbot_ref[prow:prow + 2 * half, :] = jnp.concatenate(bots, axis=0).astype(BF16)
        grows = slice(h0 * nb * half, (h0 + group) * nb * half)
        a_top = _dot(ptop_ref[grows, :], sel_ref[0:half * dh, :])
        a_bot = _dot(pbot_ref[grows, :], sel_ref[...])

        for h in range(h0, h0 + group):
            cols = head_cols[h]
            q3 = q_ref[:, cols].reshape(nb, sub, dh)
            k3 = k_ref[:, cols].reshape(nb, sub, dh)
            c3 = cum_ref[:, cols].reshape(nb, sub, dh)
            last = c3[:, sub - 1:sub, :]
            k_side = (k3 * jnp.exp2(last - c3)).reshape(ts, dh).astype(BF16)
            q_side = []
            for delta in range(1, per):
                ref_c = jnp.concatenate([jnp.zeros((delta, 1, dh), F32), last[:nb - delta]], axis=0)
                expo = jnp.where(blk >= delta, c3 - ref_c, 0.0)
                q_side.append((q3 * jnp.exp2(expo)).reshape(ts, dh))
            cross = _dot_nt(jnp.concatenate(q_side, axis=0).astype(BF16), k_side)

            rows = slice((h - h0) * nb * half, (h - h0 + 1) * nb * half)
            within = jnp.concatenate([a_top[rows].reshape(nb, half, ts), a_bot[rows].reshape(nb, half, ts)],
                                     axis=1).reshape(ts, ts)
            scores = jnp.where(dist_is[0], within, 0.0)
            for delta in range(1, per):
                scores = jnp.where(dist_is[delta], cross[(delta - 1) * ts:delta * ts], scores)
            o = o_ref[:, cols] + _dot(scores.astype(BF16), v_ref[:, cols])
            act_ref[:, cols] = (_rms(o, hn_ref[:, cols]) * og_ref[:, cols]).astype(BF16)

    y_ref[...] = x + _dot(act_ref[...], wout_ref[...])


def _hgrn_mixer(h, norm_g, w_in, lb_logits, head_norm, w_out, layer, slot):
    b, s, d = h.shape
    ts, dh = HGRN_TILE, HGRN_HEAD_DIM
    assert s % ts == 0 and d % dh == 0 and ts % HGRN_CHUNK == 0 and HGRN_CHUNK % HGRN_SUB == 0
    heads = d // dh
    n_slots = lb_logits.shape[0]
    sel, dist = _hgrn_tables()
    prow = heads * (ts // HGRN_SUB) * (HGRN_SUB // 2)
    tile_spec = pl.BlockSpec((None, ts, d), lambda bi, si: (bi, si, 0))
    return pl.pallas_call(
        functools.partial(_hgrn_kernel, layer_slot=slot),
        grid=(b, s // ts),
        in_specs=[tile_spec, _layer_spec((1, d), layer), _layer_spec((d, 4 * d), slot),
                  _const_spec((n_slots, d)), _layer_spec((1, d), slot), _layer_spec((d, d), slot),
                  _const_spec(sel.shape), _const_spec(dist.shape)],
        out_specs=tile_spec,
        out_shape=jax.ShapeDtypeStruct((b, s, d), F32),
        scratch_shapes=[pltpu.VMEM((ts, d), F32), pltpu.VMEM((ts, d), F32), pltpu.VMEM((ts, d), BF16),
                        pltpu.VMEM((ts, d), F32), pltpu.VMEM((ts, d), F32), pltpu.VMEM((ts, d), F32),
                        pltpu.VMEM((prow, (HGRN_SUB // 2) * dh), BF16), pltpu.VMEM((prow, HGRN_SUB * dh), BF16),
                        pltpu.VMEM((ts, d), BF16), pltpu.VMEM((heads, dh, dh), F32)],
        compiler_params=_params("parallel", "arbitrary"),
        name="hgrn_mixer",
    )(h, norm_g, w_in, lb_logits, head_norm, w_out, sel, dist)


def _conv_kernel(x_ref, g_ref, win_ref, bin_ref, dwx_ref, dwb_ref, lng_ref, lnb_ref, wout_ref, bout_ref,
                 y_ref, u_ref, c_ref):
    ts, d = x_ref.shape
    halo, sl = CONV_HALO, SUBLANES

    @pl.when(pl.program_id(1) == 0)
    def _():
        u_ref[0:halo, :] = jnp.zeros((halo, d), F32)

    x = x_ref[...]
    xn = _rms(x, g_ref[...]).astype(BF16)
    a = _dot(xn, win_ref[:, 0:d]) + bin_ref[:, 0:d]
    gate = _dot(xn, win_ref[:, d:2 * d]) + bin_ref[:, d:2 * d]
    u_ref[halo:halo + ts, :] = a * jax.nn.sigmoid(gate)

    sub_iota = lax.broadcasted_iota(jnp.int32, (sl, LANES), 0)
    depth = halo // sl
    for col in range(d // LANES):
        cs = slice(col * LANES, (col + 1) * LANES)
        bias = jnp.broadcast_to(dwb_ref[:, cs], (sl, LANES))
        window = [u_ref[i * sl:(i + 1) * sl, cs] for i in range(depth - 1)]
        prev = [None] * sl
        for m in range(-1, ts // sl):
            row0 = halo + m * sl
            window = window[-(depth - 1):] + [u_ref[row0:row0 + sl, cs]]
            acc = bias
            for c in range(sl):
                if m < 0 and c == 0:
                    continue
                part = None
                for a_i in range(depth):
                    r = a_i * sl + c
                    if r >= CONV_WIDTH:
                        continue
                    j = CONV_WIDTH - 1 - r
                    term = dwx_ref[j * sl:(j + 1) * sl, cs] * window[depth - 1 - a_i]
                    part = term if part is None else part + term
                if c == 0:
                    acc = acc + part
                else:
                    rolled = pltpu.roll(part, c, axis=0)
                    if m >= 0:
                        acc = acc + jnp.where(sub_iota >= c, rolled, prev[c])
                    prev[c] = rolled
            if m >= 0:
                c_ref[m * sl:(m + 1) * sl, cs] = acc
    u_ref[0:halo, :] = u_ref[ts:ts + halo, :]

    c = c_ref[...]
    mu = jnp.mean(c, axis=-1, keepdims=True)
    cc = c - mu
    var = jnp.mean(cc * cc, axis=-1, keepdims=True)
    z = cc * lax.rsqrt(var + LN_EPS) * lng_ref[...] + lnb_ref[...]
    z = (z * jax.nn.sigmoid(z)).astype(BF16)
    y_ref[...] = x + _dot(z, wout_ref[...]) + bout_ref[...]


def _conv_mixer(h, norm_g, w_in, b_in, dw, dw_b, ln_g, ln_b, w_out, b_out, layer, slot):
    b, s, d = h.shape
    ts = CONV_ROWS
    assert s % ts == 0 and ts % SUBLANES == 0 and ts >= CONV_HALO
    dwx = jnp.repeat(dw[slot], SUBLANES, axis=0)
    tile_spec = pl.BlockSpec((None, ts, d), lambda bi, si: (bi, si, 0))
    vec = lambda: _layer_spec((1, d), slot)
    return pl.pallas_call(
        _conv_kernel,
        grid=(b, s // ts),
        in_specs=[tile_spec, _layer_spec((1, d), layer), _layer_spec((d, 2 * d), slot),
                  _layer_spec((1, 2 * d), slot), _const_spec((CONV_WIDTH * SUBLANES, d)), vec(), vec(),
                  vec(), _layer_spec((d, d), slot), vec()],
        out_specs=tile_spec,
        out_shape=jax.ShapeDtypeStruct((b, s, d), F32),
        scratch_shapes=[pltpu.VMEM((ts + CONV_HALO, d), F32), pltpu.VMEM((ts, d), F32)],
        compiler_params=_params("parallel", "arbitrary"),
        name="conv_mixer",
    )(h, norm_g, w_in, b_in, dwx, dw_b, ln_g, ln_b, w_out, b_out)


def _memkv_kernel(m_ref, g_ref, w_ref, kv_ref):
    mn = _rms(m_ref[...], g_ref[...]).astype(BF16)
    kv_ref[...] = _dot(mn, w_ref[...]).astype(BF16)


def _memkv(mem2d, mem_norm, wkv):
    n, d = mem2d.shape
    n_layers = wkv.shape[0]
    rows = min(ROW_TILE, n)
    return pl.pallas_call(
        _memkv_kernel,
        grid=(n_layers, n // rows),
        in_specs=[pl.BlockSpec((rows, d), lambda li, i: (i, 0)), _const_spec((1, d)),
                  pl.BlockSpec((None, d, 2 * d), lambda li, i: (li, 0, 0))],
        out_specs=pl.BlockSpec((None, rows, 2 * d), lambda li, i: (li, i, 0)),
        out_shape=jax.ShapeDtypeStruct((n_layers, n, 2 * d), BF16),
        compiler_params=_params("arbitrary", "arbitrary"),
        name="mem_kv",
    )(mem2d, mem_norm.reshape(1, d), wkv)


def _xattn_kernel(x_ref, g_ref, wq_ref, kv_ref, wo_ref, y_ref, act_ref, *, heads):
    ts, d = x_ref.shape
    dh = d // heads
    scale = dh ** -0.5
    x = x_ref[...]
    xn = _rms(x, g_ref[...]).astype(BF16)
    q = _dot(xn, wq_ref[...]).astype(BF16)
    for h in range(heads):
        cols = slice(h * dh, (h + 1) * dh)
        s = _dot_nt(q[:, cols], kv_ref[:, cols]) * scale
        s = s - jnp.max(s, axis=-1, keepdims=True)
        p = jnp.exp(s)
        p = p / jnp.sum(p, axis=-1, keepdims=True)
        act_ref[:, cols] = _dot(p.astype(BF16), kv_ref[:, d + h * dh:d + (h + 1) * dh]).astype(BF16)
    y_ref[...] = x + _dot(act_ref[...], wo_ref[...])


def _xattn(h, norm_g, wq, kv, wo, layer, heads):
    b, s, d = h.shape
    m = kv.shape[2]
    ts = XATTN_ROWS
    tile_spec = pl.BlockSpec((None, ts, d), lambda bi, si: (bi, si, 0))
    return pl.pallas_call(
        functools.partial(_xattn_kernel, heads=heads),
        grid=(b, s // ts),
        in_specs=[tile_spec, _layer_spec((1, d), layer), _layer_spec((d, d), layer),
                  pl.BlockSpec((None, None, m, 2 * d), lambda bi, si: (layer, bi, 0, 0)),
                  _layer_spec((d, d), layer)],
        out_specs=tile_spec,
        out_shape=jax.ShapeDtypeStruct((b, s, d), F32),
        scratch_shapes=[pltpu.VMEM((ts, d), BF16)],
        compiler_params=_params("parallel", "parallel"),
        name="xattn",
    )(h, norm_g, wq, kv, wo)


def kernel(x, mem, ffn1_norm, ffn1_w_in, ffn1_w_out, mix_norm, hgrn_w_in, hgrn_head_norm, hgrn_w_out,
           hgrn_lb_logits, conv_w_in, conv_b_in, conv_dw, conv_dw_b, conv_ln_g, conv_ln_b, conv_w_out,
           conv_b_out, xattn_norm, xattn_wq, xattn_wkv, xattn_wo, ffn2_norm, ffn2_w_in, ffn2_w_out,
           mem_norm, final_norm):
    b, s, d = x.shape
    m = mem.shape[1]
    depth = ffn1_norm.shape[0]
    t = b * s
    bf = lambda w: w.astype(BF16)
    vec = lambda a: a.reshape(a.shape[0], 1, a.shape[1])
    flat = lambda a: a.reshape(t, d)
    cube = lambda a: a.reshape(b, s, d)

    ffn1_w_in, ffn1_w_out, ffn2_w_in, ffn2_w_out = bf(ffn1_w_in), bf(ffn1_w_out), bf(ffn2_w_in), bf(ffn2_w_out)
    hgrn_w_in, hgrn_w_out = bf(hgrn_w_in), bf(hgrn_w_out)
    conv_w_in, conv_w_out = bf(conv_w_in), bf(conv_w_out)
    xattn_wq, xattn_wo = bf(xattn_wq), bf(xattn_wo)
    ffn1_norm, ffn2_norm, mix_norm, xattn_norm = vec(ffn1_norm), vec(ffn2_norm), vec(mix_norm), vec(xattn_norm)

    kv = _memkv(mem.reshape(b * m, d), mem_norm, bf(xattn_wkv)).reshape(depth, b, m, 2 * d)
    h = x
    for layer in range(depth):
        h = cube(_ffn(flat(h), ffn1_norm, ffn1_w_in, ffn1_w_out, layer))
        j = layer // 2
        if layer % 2 == 0:
            h = _hgrn_mixer(h, mix_norm, hgrn_w_in, hgrn_lb_logits, vec(hgrn_head_norm), hgrn_w_out, layer, j)
        else:
            h = _conv_mixer(h, mix_norm, conv_w_in, vec(conv_b_in), conv_dw, vec(conv_dw_b), vec(conv_ln_g),
                            vec(conv_ln_b), conv_w_out, vec(conv_b_out), layer, j)
        h = _xattn(h, xattn_norm, xattn_wq, kv, xattn_wo, layer, XATTN_HEADS)
        last = layer == depth - 1
        h = cube(_ffn(flat(h), ffn2_norm, ffn2_w_in, ffn2_w_out, layer, final_norm if last else None))
    return h
```

```python
import functools

import numpy as np
import jax
import jax.numpy as jnp
from jax import lax
from jax.experimental import pallas as pl
from jax.experimental.pallas import tpu as pltpu

F32 = jnp.float32
BF16 = jnp.bfloat16

RMS_EPS = 1e-6
LN_EPS = 1e-5

HGRN_HEAD_DIM = 128
XATTN_HEADS = 4
CONV_WIDTH = 31

LANES = 128
SUBLANES = 8
VMEM_LIMIT_BYTES = 56 * 1024 * 1024

HGRN_TILE = 2 * LANES
HGRN_CHUNK = 64
HGRN_SUB = 2 * SUBLANES

FFN_ROWS = 1024
FFN_COLS = 256
ROW_TILE = 512
XATTN_ROWS = 1024
CONV_ROWS = 256
CONV_UNIT = 2 * SUBLANES
CONV_TAP_TILES = -(-CONV_WIDTH // SUBLANES)
CONV_HALO = CONV_UNIT + (CONV_TAP_TILES - 1) * SUBLANES
CONV_PHASE_ROWS = CONV_ROWS + CONV_UNIT
CONV_SHIFT_K = -(-((SUBLANES - 1) * CONV_PHASE_ROWS) // LANES) * LANES


def _const_spec(shape):
    zeros = (0,) * len(shape)
    return pl.BlockSpec(shape, lambda *_: zeros, pipeline_mode=pl.Buffered(1))


def _layer_spec(shape, layer):
    zeros = (0,) * len(shape)
    return pl.BlockSpec((None,) + tuple(shape), lambda *_: (layer,) + zeros,
                        pipeline_mode=pl.Buffered(1))


def _params(*semantics):
    return pltpu.CompilerParams(dimension_semantics=semantics,
                                vmem_limit_bytes=VMEM_LIMIT_BYTES)


def _rms(x, g):
    ms = jnp.mean(x * x, axis=-1, keepdims=True)
    return x * lax.rsqrt(ms + RMS_EPS) * g


def _dot(a, b):
    return jnp.dot(a, b, preferred_element_type=F32)


def _dot_nt(a, b):
    return lax.dot_general(a, b, (((1,), (1,)), ((), ())), preferred_element_type=F32)


def _dot_tn(a, b):
    return lax.dot_general(a, b, (((0,), (0,)), ((), ())), preferred_element_type=F32)


def _ffn_kernel(*refs, d_ff, final):
    if final:
        x_ref, g_ref, win_ref, wout_ref, fg_ref, o_ref, act_ref = refs
    else:
        x_ref, g_ref, win_ref, wout_ref, o_ref, act_ref = refs
    x = x_ref[...]
    xn = _rms(x, g_ref[...]).astype(BF16)
    for c in range(d_ff // FFN_COLS):
        lo = c * FFN_COLS
        gate = _dot(xn, win_ref[:, lo:lo + FFN_COLS])
        up = _dot(xn, win_ref[:, d_ff + lo:d_ff + lo + FFN_COLS])
        act_ref[:, lo:lo + FFN_COLS] = (gate * jax.nn.sigmoid(gate) * up).astype(BF16)
    y = x + 0.5 * _dot(act_ref[...], wout_ref[...])
    if final:
        y = _rms(y, fg_ref[...])
    o_ref[...] = y


def _ffn(h, norm_g, w_in, w_out, layer, final_g=None):
    t, d = h.shape
    d_ff = w_out.shape[1]
    assert t % FFN_ROWS == 0 and d_ff % FFN_COLS == 0
    final = final_g is not None
    row_spec = pl.BlockSpec((FFN_ROWS, d), lambda i: (i, 0))
    in_specs = [row_spec, _layer_spec((1, d), layer), _layer_spec((d, 2 * d_ff), layer),
                _layer_spec((d_ff, d), layer)]
    args = [h, norm_g, w_in, w_out]
    if final:
        in_specs.append(_const_spec((1, d)))
        args.append(final_g.reshape(1, d))
    return pl.pallas_call(
        functools.partial(_ffn_kernel, d_ff=d_ff, final=final),
        grid=(t // FFN_ROWS,),
        in_specs=in_specs,
        out_specs=row_spec,
        out_shape=jax.ShapeDtypeStruct((t, d), F32),
        scratch_shapes=[pltpu.VMEM((FFN_ROWS, d_ff), BF16)],
        compiler_params=_params("parallel"),
        name="ffn",
    )(*args)


def _hgrn_tables():
    sel = np.zeros((HGRN_SUB * LANES, HGRN_TILE), np.float32)
    for s in range(HGRN_SUB):
        sel[s * LANES:(s + 1) * LANES, s::HGRN_SUB] = 1.0
    r = np.arange(HGRN_TILE)
    same_chunk = (r[:, None] // HGRN_CHUNK) == (r[None, :] // HGRN_CHUNK)
    dist = r[:, None] // HGRN_SUB - r[None, :] // HGRN_SUB
    dist = np.where(same_chunk & (dist >= 0), dist, -1).astype(np.int32)
    return jnp.asarray(sel, BF16), jnp.asarray(dist)


def _hgrn_kernel(x_ref, g_ref, win_ref, lbl_ref, hn_ref, wout_ref, sel_ref, dist_ref, y_ref,
                 q_ref, k_ref, v_ref, cum_ref, og_ref, o_ref, ptop_ref, pbot_ref, act_ref, st_ref, *, layer_slot):
    ts, d = x_ref.shape
    dh, sub, chunk = HGRN_HEAD_DIM, HGRN_SUB, HGRN_CHUNK
    heads, nb, nc, per = d // dh, ts // sub, ts // chunk, chunk // sub
    half = sub // 2
    head_cols = [slice(h * dh, (h + 1) * dh) for h in range(heads)]

    @pl.when(pl.program_id(1) == 0)
    def _():
        st_ref[...] = jnp.zeros_like(st_ref)

    x = x_ref[...]
    xn = _rms(x, g_ref[...]).astype(BF16)
    logits = lbl_ref[...]
    e = jnp.exp(logits - jnp.max(logits, axis=0, keepdims=True))
    sm = e / jnp.sum(e, axis=0, keepdims=True)
    lb = jnp.sum(sm[0:layer_slot + 1], axis=0, keepdims=True)

    qp = _dot(xn, win_ref[:, 0:d])
    q_ref[...] = qp * jax.nn.sigmoid(qp)
    fp = _dot(xn, win_ref[:, d:2 * d])
    k_ref[...] = (1.0 - lb) * jax.nn.sigmoid(-fp)
    cum = jnp.log2(lb + (1.0 - lb) * jax.nn.sigmoid(fp))
    row = lax.broadcasted_iota(jnp.int32, (ts, d), 0) % chunk
    sh = 1
    while sh < chunk:
        cum = cum + jnp.where(row >= sh, pltpu.roll(cum, sh, axis=0), 0.0)
        sh *= 2
    cum_ref[...] = cum
    v_ref[...] = _dot(xn, win_ref[:, 2 * d:3 * d]).astype(BF16)
    gp = _dot(xn, win_ref[:, 3 * d:4 * d])
    og_ref[...] = gp * jax.nn.sigmoid(gp)

    q_dec, k_dec, c_end = [], [], []
    for cols in head_cols:
        cum_h = cum_ref[:, cols]
        c4 = cum_h.reshape(nc, chunk, dh)
        end = c4[:, chunk - 1:chunk, :]
        c_end.append(end)
        q_dec.append((q_ref[:, cols] * jnp.exp2(cum_h)).astype(BF16))
        k_dec.append((k_ref[:, cols].reshape(nc, chunk, dh) * jnp.exp2(end - c4)).astype(BF16))
    states = [st_ref[h] for h in range(heads)]
    for c in range(nc):
        crow = slice(c * chunk, (c + 1) * chunk)
        for h, cols in enumerate(head_cols):
            o_ref[crow, cols] = _dot_nt(q_dec[h][crow], states[h].astype(BF16))
            states[h] = states[h] * jnp.exp2(c_end[h][c]) + _dot_tn(v_ref[crow, cols], k_dec[h][c])
    for h in range(heads):
        st_ref[h] = states[h]

    dist = dist_ref[...]
    dist_is = [dist == delta for delta in range(per)]
    blk = lax.broadcasted_iota(jnp.int32, (nb, 1, dh), 0) % per
    sub_iota = lax.broadcasted_iota(jnp.int32, (half, dh), 0)
    group = 2
    for h0 in range(0, heads, group):
        for h in range(h0, h0 + group):
            cols = head_cols[h]
            for b0 in range(0, nb, 2):
                tops, bots = [], []
                for r0 in (b0 * sub, (b0 + 1) * sub):
                    q_top, q_bot = q_ref[r0:r0 + half, cols], q_ref[r0 + half:r0 + sub, cols]
                    c_top, c_bot = cum_ref[r0:r0 + half, cols], cum_ref[r0 + half:r0 + sub, cols]
                    p_top, p_bot = [], []
                    for s in range(sub):
                        c_s = cum_ref[r0 + s:r0 + s + 1, cols]
                        k_s = k_ref[r0 + s:r0 + s + 1, cols]
                        if s < half:
                            d_top = c_top - c_s
                            if s > 0:
                                d_top = jnp.where(sub_iota >= s, d_top, -jnp.inf)
                            p_top.append(jnp.exp2(d_top) * q_top * k_s)
                            p_bot.append(jnp.exp2(c_bot - c_s) * q_bot * k_s)
                        else:
                            d_bot = c_bot - c_s
                            if s > half:
                                d_bot = jnp.where(sub_iota >= s - half, d_bot, -jnp.inf)
                            p_bot.append(jnp.exp2(d_bot) * q_bot * k_s)
                    tops.append(jnp.concatenate(p_top, axis=-1))
                    bots.append(jnp.concatenate(p_bot, axis=-1))
                prow = (h * nb + b0) * half
                ptop_ref[prow:prow + 2 * half, :] = jnp.concatenate(tops, axis=0).astype(BF16)
                pbot_ref[prow:prow + 2 * half, :] = jnp.concatenate(bots, axis=0).astype(BF16)
        grows = slice(h0 * nb * half, (h0 + group) * nb * half)
        a_top = _dot(ptop_ref[grows, :], sel_ref[0:half * dh, :])
        a_bot = _dot(pbot_ref[grows, :], sel_ref[...])

        for h in range(h0, h0 + group):
            cols = head_cols[h]
            q3 = q_ref[:, cols].reshape(nb, sub, dh)
            k3 = k_ref[:, cols].reshape(nb, sub, dh)
            c3 = cum_ref[:, cols].reshape(nb, sub, dh)
            last = c3[:, sub - 1:sub, :]
            k_side = (k3 * jnp.exp2(last - c3)).reshape(ts, dh).astype(BF16)
            q_side = []
            for delta in range(1, per):
                ref_c = jnp.concatenate([jnp.zeros((delta, 1, dh), F32), last[:nb - delta]], axis=0)
                expo = jnp.where(blk >= delta, c3 - ref_c, 0.0)
                q_side.append((q3 * jnp.exp2(expo)).reshape(ts, dh))
            cross = _dot_nt(jnp.concatenate(q_side, axis=0).astype(BF16), k_side)

            rows = slice((h - h0) * nb * half, (h - h0 + 1) * nb * half)
            within = jnp.concatenate([a_top[rows].reshape(nb, half, ts), a_bot[rows].reshape(nb, half, ts)],
                                     axis=1).reshape(ts, ts)
            scores = jnp.where(dist_is[0], within, 0.0)
            for delta in range(1, per):
                scores = jnp.where(dist_is[delta], cross[(delta - 1) * ts:delta * ts], scores)
            o = o_ref[:, cols] + _dot(scores.astype(BF16), v_ref[:, cols])
            act_ref[:, cols] = (_rms(o, hn_ref[:, cols]) * og_ref[:, cols]).astype(BF16)

    y_ref[...] = x + _dot(act_ref[...], wout_ref[...])


def _hgrn_mixer(h, norm_g, w_in, lb_logits, head_norm, w_out, layer, slot):
    b, s, d = h.shape
    ts, dh = HGRN_TILE, HGRN_HEAD_DIM
    assert s % ts == 0 and d % dh == 0 and ts % HGRN_CHUNK == 0 and HGRN_CHUNK % HGRN_SUB == 0
    heads = d // dh
    n_slots = lb_logits.shape[0]
    sel, dist = _hgrn_tables()
    prow = heads * (ts // HGRN_SUB) * (HGRN_SUB // 2)
    tile_spec = pl.BlockSpec((None, ts, d), lambda bi, si: (bi, si, 0))
    return pl.pallas_call(
        functools.partial(_hgrn_kernel, layer_slot=slot),
        grid=(b, s // ts),
        in_specs=[tile_spec, _layer_spec((1, d), layer), _layer_spec((d, 4 * d), slot),
                  _const_spec((n_slots, d)), _layer_spec((1, d), slot), _layer_spec((d, d), slot),
                  _const_spec(sel.shape), _const_spec(dist.shape)],
        out_specs=tile_spec,
        out_shape=jax.ShapeDtypeStruct((b, s, d), F32),
        scratch_shapes=[pltpu.VMEM((ts, d), F32), pltpu.VMEM((ts, d), F32), pltpu.VMEM((ts, d), BF16),
                        pltpu.VMEM((ts, d), F32), pltpu.VMEM((ts, d), F32), pltpu.VMEM((ts, d), F32),
                        pltpu.VMEM((prow, (HGRN_SUB // 2) * dh), BF16), pltpu.VMEM((prow, HGRN_SUB * dh), BF16),
                        pltpu.VMEM((ts, d), BF16), pltpu.VMEM((heads, dh, dh), F32)],
        compiler_params=_params("parallel", "arbitrary"),
        name="hgrn_mixer",
    )(h, norm_g, w_in, lb_logits, head_norm, w_out, sel, dist)


def _conv_shift_matrix():
    shift = np.zeros((CONV_ROWS, CONV_SHIFT_K), np.float32)
    t = np.arange(CONV_ROWS)
    for c in range(1, SUBLANES):
        shift[t, (c - 1) * CONV_PHASE_ROWS + CONV_UNIT + t - c] = 1.0
    return jnp.asarray(shift, BF16)


def _conv_kernel(x_ref, g_ref, win_ref, bin_ref, dwx_ref, dwb_ref, lng_ref, lnb_ref, wout_ref, bout_ref,
                 shift_ref, y_ref, u_ref, c_ref, v_ref):
    ts, d = x_ref.shape
    halo, sl, unit, depth = CONV_HALO, SUBLANES, CONV_UNIT, CONV_TAP_TILES

    @pl.when(pl.program_id(1) == 0)
    def _():
        u_ref[0:halo, :] = jnp.zeros((halo, d), F32)
        pad = CONV_SHIFT_K - (sl - 1) * CONV_PHASE_ROWS
        v_ref[CONV_SHIFT_K - pad:CONV_SHIFT_K, :] = jnp.zeros((pad, d), BF16)

    x = x_ref[...]
    xn = _rms(x, g_ref[...]).astype(BF16)
    a = _dot(xn, win_ref[:, 0:d]) + bin_ref[:, 0:d]
    gate = _dot(xn, win_ref[:, d:2 * d]) + bin_ref[:, d:2 * d]
    u_ref[halo:halo + ts, :] = a * jax.nn.sigmoid(gate)

    for col in range(d // LANES):
        cs = slice(col * LANES, (col + 1) * LANES)
        bias = jnp.broadcast_to(dwb_ref[:, cs], (sl, LANES))
        for m in range(-1, ts // unit):
            base = halo + m * unit - (depth - 1) * sl
            window = [u_ref[base + j * sl:base + (j + 1) * sl, cs] for j in range(depth + 1)]
            for c in range(sl):
                if m < 0 and c == 0:
                    continue
                top, bot = (bias, bias) if c == 0 else (None, None)
                for a_i in range(depth):
                    r = a_i * sl + c
                    if r >= CONV_WIDTH:
                        continue
                    j = CONV_WIDTH - 1 - r
                    tap = dwx_ref[j * sl:(j + 1) * sl, cs]
                    t_term = tap * window[depth - 1 - a_i]
                    b_term = tap * window[depth - a_i]
                    top = t_term if top is None else top + t_term
                    bot = b_term if bot is None else bot + b_term
                both = jnp.concatenate([top, bot], axis=0)
                if c == 0:
                    c_ref[m * unit:(m + 1) * unit, cs] = both
                else:
                    row0 = (c - 1) * CONV_PHASE_ROWS + (m + 1) * unit
                    v_ref[row0:row0 + unit, cs] = both.astype(BF16)
    u_ref[0:halo, :] = u_ref[ts:ts + halo, :]

    c = c_ref[...] + _dot(shift_ref[...], v_ref[...])
    mu = jnp.mean(c, axis=-1, keepdims=True)
    cc = c - mu
    var = jnp.mean(cc * cc, axis=-1, keepdims=True)
    z = cc * lax.rsqrt(var + LN_EPS) * lng_ref[...] + lnb_ref[...]
    z = (z * jax.nn.sigmoid(z)).astype(BF16)
    y_ref[...] = x + _dot(z, wout_ref[...]) + bout_ref[...]


def _conv_mixer(h, norm_g, w_in, b_in, dw, dw_b, ln_g, ln_b, w_out, b_out, layer, slot):
    b, s, d = h.shape
    ts = CONV_ROWS
    assert s % ts == 0 and ts % CONV_UNIT == 0 and ts >= CONV_HALO
    dwx = jnp.repeat(dw[slot], SUBLANES, axis=0)
    shift = _conv_shift_matrix()
    tile_spec = pl.BlockSpec((None, ts, d), lambda bi, si: (bi, si, 0))
    vec = lambda: _layer_spec((1, d), slot)
    return pl.pallas_call(
        _conv_kernel,
        grid=(b, s // ts),
        in_specs=[tile_spec, _layer_spec((1, d), layer), _layer_spec((d, 2 * d), slot),
                  _layer_spec((1, 2 * d), slot), _const_spec((CONV_WIDTH * SUBLANES, d)), vec(), vec(),
                  vec(), _layer_spec((d, d), slot), vec(), _const_spec(shift.shape)],
        out_specs=tile_spec,
        out_shape=jax.ShapeDtypeStruct((b, s, d), F32),
        scratch_shapes=[pltpu.VMEM((ts + CONV_HALO, d), F32), pltpu.VMEM((ts, d), F32),
                        pltpu.VMEM((CONV_SHIFT_K, d), BF16)],
        compiler_params=_params("parallel", "arbitrary"),
        name="conv_mixer",
    )(h, norm_g, w_in, b_in, dwx, dw_b, ln_g, ln_b, w_out, b_out, shift)


def _memkv_kernel(m_ref, g_ref, w_ref, kv_ref):
    mn = _rms(m_ref[...], g_ref[...]).astype(BF16)
    kv_ref[...] = _dot(mn, w_ref[...]).astype(BF16)


def _memkv(mem2d, mem_norm, wkv):
    n, d = mem2d.shape
    n_layers = wkv.shape[0]
    rows = min(ROW_TILE, n)
    return pl.pallas_call(
        _memkv_kernel,
        grid=(n_layers, n // rows),
        in_specs=[pl.BlockSpec((rows, d), lambda li, i: (i, 0)), _const_spec((1, d)),
                  pl.BlockSpec((None, d, 2 * d), lambda li, i: (li, 0, 0))],
        out_specs=pl.BlockSpec((None, rows, 2 * d), lambda li, i: (li, i, 0)),
        out_shape=jax.ShapeDtypeStruct((n_layers, n, 2 * d), BF16),
        compiler_params=_params("arbitrary", "arbitrary"),
        name="mem_kv",
    )(mem2d, mem_norm.reshape(1, d), wkv)


def _xattn_kernel(x_ref, g_ref, wq_ref, kv_ref, wo_ref, y_ref, act_ref, *, heads):
    ts, d = x_ref.shape
    dh = d // heads
    scale = dh ** -0.5
    x = x_ref[...]
    xn = _rms(x, g_ref[...]).astype(BF16)
    q = _dot(xn, wq_ref[...]).astype(BF16)
    for h in range(heads):
        cols = slice(h * dh, (h + 1) * dh)
        s = _dot_nt(q[:, cols], kv_ref[:, cols]) * scale
        s = s - jnp.max(s, axis=-1, keepdims=True)
        p = jnp.exp(s)
        p = p / jnp.sum(p, axis=-1, keepdims=True)
        act_ref[:, cols] = _dot(p.astype(BF16), kv_ref[:, d + h * dh:d + (h + 1) * dh]).astype(BF16)
    y_ref[...] = x + _dot(act_ref[...], wo_ref[...])


def _xattn(h, norm_g, wq, kv, wo, layer, heads):
    b, s, d = h.shape
    m = kv.shape[2]
    ts = XATTN_ROWS
    tile_spec = pl.BlockSpec((None, ts, d), lambda bi, si: (bi, si, 0))
    return pl.pallas_call(
        functools.partial(_xattn_kernel, heads=heads),
        grid=(b, s // ts),
        in_specs=[tile_spec, _layer_spec((1, d), layer), _layer_spec((d, d), layer),
                  pl.BlockSpec((None, None, m, 2 * d), lambda bi, si: (layer, bi, 0, 0)),
                  _layer_spec((d, d), layer)],
        out_specs=tile_spec,
        out_shape=jax.ShapeDtypeStruct((b, s, d), F32),
        scratch_shapes=[pltpu.VMEM((ts, d), BF16)],
        compiler_params=_params("parallel", "parallel"),
        name="xattn",
    )(h, norm_g, wq, kv, wo)


def kernel(x, mem, ffn1_norm, ffn1_w_in, ffn1_w_out, mix_norm, hgrn_w_in, hgrn_head_norm, hgrn_w_out,
           hgrn_lb_logits, conv_w_in, conv_b_in, conv_dw, conv_dw_b, conv_ln_g, conv_ln_b, conv_w_out,
           conv_b_out, xattn_norm, xattn_wq, xattn_wkv, xattn_wo, ffn2_norm, ffn2_w_in, ffn2_w_out,
           mem_norm, final_norm):
    b, s, d = x.shape
    m = mem.shape[1]
    depth = ffn1_norm.shape[0]
    t = b * s
    bf = lambda w: w.astype(BF16)
    vec = lambda a: a.reshape(a.shape[0], 1, a.shape[1])
    flat = lambda a: a.reshape(t, d)
    cube = lambda a: a.reshape(b, s, d)

    ffn1_w_in, ffn1_w_out, ffn2_w_in, ffn2_w_out = bf(ffn1_w_in), bf(ffn1_w_out), bf(ffn2_w_in), bf(ffn2_w_out)
    hgrn_w_in, hgrn_w_out = bf(hgrn_w_in), bf(hgrn_w_out)
    conv_w_in, conv_w_out = bf(conv_w_in), bf(conv_w_out)
    xattn_wq, xattn_wo = bf(xattn_wq), bf(xattn_wo)
    ffn1_norm, ffn2_norm, mix_norm, xattn_norm = vec(ffn1_norm), vec(ffn2_norm), vec(mix_norm), vec(xattn_norm)

    kv = _memkv(mem.reshape(b * m, d), mem_norm, bf(xattn_wkv)).reshape(depth, b, m, 2 * d)
    h = x
    for layer in range(depth):
        h = cube(_ffn(flat(h), ffn1_norm, ffn1_w_in, ffn1_w_out, layer))
        j = layer // 2
        if layer % 2 == 0:
            h = _hgrn_mixer(h, mix_norm, hgrn_w_in, hgrn_lb_logits, vec(hgrn_head_norm), hgrn_w_out, layer, j)
        else:
            h = _conv_mixer(h, mix_norm, conv_w_in, vec(conv_b_in), conv_dw, vec(conv_dw_b), vec(conv_ln_g),
                            vec(conv_ln_b), conv_w_out, vec(conv_b_out), layer, j)
        h = _xattn(h, xattn_norm, xattn_wq, kv, xattn_wo, layer, XATTN_HEADS)
        last = layer == depth - 1
        h = cube(_ffn(flat(h), ffn2_norm, ffn2_w_in, ffn2_w_out, layer, final_norm if last else None))
    return h
```

```python
import functools

import numpy as np
import jax
import jax.numpy as jnp
from jax import lax
from jax.experimental import pallas as pl
from jax.experimental.pallas import tpu as pltpu

F32 = jnp.float32
BF16 = jnp.bfloat16

RMS_EPS = 1e-6
LN_EPS = 1e-5

HGRN_HEAD_DIM = 128
XATTN_HEADS = 4
CONV_WIDTH = 31

LANES = 128
SUBLANES = 8
VMEM_LIMIT_BYTES = 56 * 1024 * 1024

HGRN_TILE = 2 * LANES
HGRN_CHUNK = 64
HGRN_SUB = 2 * SUBLANES

FFN_ROWS = 1024
FFN_COLS = 256
ROW_TILE = 512
XATTN_ROWS = 1024
CONV_ROWS = 256
CONV_UNIT = 2 * SUBLANES
CONV_TAP_TILES = -(-CONV_WIDTH // SUBLANES)
CONV_HALO = CONV_UNIT + (CONV_TAP_TILES - 1) * SUBLANES
CONV_PHASE_ROWS = CONV_ROWS + CONV_UNIT
CONV_SHIFT_K = -(-((SUBLANES - 1) * CONV_PHASE_ROWS) // LANES) * LANES
CONV_PAD_LANES = LANES


def _const_spec(shape):
    zeros = (0,) * len(shape)
    return pl.BlockSpec(shape, lambda *_: zeros, pipeline_mode=pl.Buffered(1))


def _layer_spec(shape, layer):
    zeros = (0,) * len(shape)
    return pl.BlockSpec((None,) + tuple(shape), lambda *_: (layer,) + zeros,
                        pipeline_mode=pl.Buffered(1))


def _params(*semantics):
    return pltpu.CompilerParams(dimension_semantics=semantics,
                                vmem_limit_bytes=VMEM_LIMIT_BYTES)


def _rms(x, g):
    ms = jnp.mean(x * x, axis=-1, keepdims=True)
    return x * lax.rsqrt(ms + RMS_EPS) * g


def _dot(a, b):
    return jnp.dot(a, b, preferred_element_type=F32)


def _dot_nt(a, b):
    return lax.dot_general(a, b, (((1,), (1,)), ((), ())), preferred_element_type=F32)


def _dot_tn(a, b):
    return lax.dot_general(a, b, (((0,), (0,)), ((), ())), preferred_element_type=F32)


def _ffn_kernel(*refs, d_ff, final):
    if final:
        x_ref, g_ref, win_ref, wout_ref, fg_ref, o_ref, act_ref = refs
    else:
        x_ref, g_ref, win_ref, wout_ref, o_ref, act_ref = refs
    x = x_ref[...]
    xn = _rms(x, g_ref[...]).astype(BF16)
    for c in range(d_ff // FFN_COLS):
        lo = c * FFN_COLS
        gate = _dot(xn, win_ref[:, lo:lo + FFN_COLS])
        up = _dot(xn, win_ref[:, d_ff + lo:d_ff + lo + FFN_COLS])
        act_ref[:, lo:lo + FFN_COLS] = (gate * jax.nn.sigmoid(gate) * up).astype(BF16)
    y = x + 0.5 * _dot(act_ref[...], wout_ref[...])
    if final:
        y = _rms(y, fg_ref[...])
    o_ref[...] = y


def _ffn(h, norm_g, w_in, w_out, layer, final_g=None):
    t, d = h.shape
    d_ff = w_out.shape[1]
    assert t % FFN_ROWS == 0 and d_ff % FFN_COLS == 0
    final = final_g is not None
    row_spec = pl.BlockSpec((FFN_ROWS, d), lambda i: (i, 0))
    in_specs = [row_spec, _layer_spec((1, d), layer), _layer_spec((d, 2 * d_ff), layer),
                _layer_spec((d_ff, d), layer)]
    args = [h, norm_g, w_in, w_out]
    if final:
        in_specs.append(_const_spec((1, d)))
        args.append(final_g.reshape(1, d))
    return pl.pallas_call(
        functools.partial(_ffn_kernel, d_ff=d_ff, final=final),
        grid=(t // FFN_ROWS,),
        in_specs=in_specs,
        out_specs=row_spec,
        out_shape=jax.ShapeDtypeStruct((t, d), F32),
        scratch_shapes=[pltpu.VMEM((FFN_ROWS, d_ff), BF16)],
        compiler_params=_params("parallel"),
        name="ffn",
    )(*args)


def _hgrn_tables():
    sel = np.zeros((HGRN_SUB * LANES, HGRN_TILE), np.float32)
    for s in range(HGRN_SUB):
        sel[s * LANES:(s + 1) * LANES, s::HGRN_SUB] = 1.0
    r = np.arange(HGRN_TILE)
    same_chunk = (r[:, None] // HGRN_CHUNK) == (r[None, :] // HGRN_CHUNK)
    dist = r[:, None] // HGRN_SUB - r[None, :] // HGRN_SUB
    dist = np.where(same_chunk & (dist >= 0), dist, -1).astype(np.int32)
    return jnp.asarray(sel, BF16), jnp.asarray(dist)


def _hgrn_kernel(x_ref, g_ref, win_ref, lbl_ref, hn_ref, wout_ref, sel_ref, dist_ref, y_ref,
                 q_ref, k_ref, v_ref, cum_ref, og_ref, o_ref, ptop_ref, pbot_ref, act_ref, st_ref, *, layer_slot):
    ts, d = x_ref.shape
    dh, sub, chunk = HGRN_HEAD_DIM, HGRN_SUB, HGRN_CHUNK
    heads, nb, nc, per = d // dh, ts // sub, ts // chunk, chunk // sub
    half = sub // 2
    head_cols = [slice(h * dh, (h + 1) * dh) for h in range(heads)]

    @pl.when(pl.program_id(1) == 0)
    def _():
        st_ref[...] = jnp.zeros_like(st_ref)

    x = x_ref[...]
    xn = _rms(x, g_ref[...]).astype(BF16)
    logits = lbl_ref[...]
    e = jnp.exp(logits - jnp.max(logits, axis=0, keepdims=True))
    sm = e / jnp.sum(e, axis=0, keepdims=True)
    lb = jnp.sum(sm[0:layer_slot + 1], axis=0, keepdims=True)

    qp = _dot(xn, win_ref[:, 0:d])
    qs = qp * jax.nn.sigmoid(qp)
    fp = _dot(xn, win_ref[:, d:2 * d])
    ks = (1.0 - lb) * jax.nn.sigmoid(-fp)
    cum = jnp.log2(lb + (1.0 - lb) * jax.nn.sigmoid(fp))
    row = lax.broadcasted_iota(jnp.int32, (ts, d), 0) % chunk
    sh = 1
    while sh < chunk:
        cum = cum + jnp.where(row >= sh, pltpu.roll(cum, sh, axis=0), 0.0)
        sh *= 2
    for h, cols in enumerate(head_cols):
        q_ref[h] = qs[:, cols]
        k_ref[h] = ks[:, cols]
        cum_ref[h] = cum[:, cols]
    vp = _dot(xn, win_ref[:, 2 * d:3 * d]).astype(BF16)
    gp = _dot(xn, win_ref[:, 3 * d:4 * d])
    gs = gp * jax.nn.sigmoid(gp)
    for h, cols in enumerate(head_cols):
        v_ref[h] = vp[:, cols]
        og_ref[h] = gs[:, cols]

    q_dec, k_dec, c_end = [], [], []
    for h in range(heads):
        cum_h = cum_ref[h]
        c4 = cum_h.reshape(nc, chunk, dh)
        end = c4[:, chunk - 1:chunk, :]
        c_end.append(end)
        q_dec.append((q_ref[h] * jnp.exp2(cum_h)).astype(BF16))
        k_dec.append((k_ref[h].reshape(nc, chunk, dh) * jnp.exp2(end - c4)).astype(BF16))
    states = [st_ref[h] for h in range(heads)]
    for c in range(nc):
        crow = slice(c * chunk, (c + 1) * chunk)
        for h, cols in enumerate(head_cols):
            o_ref[h, crow, :] = _dot_nt(q_dec[h][crow], states[h].astype(BF16))
            states[h] = states[h] * jnp.exp2(c_end[h][c]) + _dot_tn(v_ref[h, crow, :], k_dec[h][c])
    for h in range(heads):
        st_ref[h] = states[h]

    dist = dist_ref[...]
    dist_is = [dist == delta for delta in range(per)]
    blk = lax.broadcasted_iota(jnp.int32, (nb, 1, dh), 0) % per
    sub_iota = lax.broadcasted_iota(jnp.int32, (half, dh), 0)
    group = 2
    for h0 in range(0, heads, group):
        for h in range(h0, h0 + group):
            cols = head_cols[h]
            for b0 in range(0, nb, 2):
                tops, bots = [], []
                for r0 in (b0 * sub, (b0 + 1) * sub):
                    q_top, q_bot = q_ref[h, r0:r0 + half, :], q_ref[h, r0 + half:r0 + sub, :]
                    c_top, c_bot = cum_ref[h, r0:r0 + half, :], cum_ref[h, r0 + half:r0 + sub, :]
                    p_top, p_bot = [], []
                    for s in range(sub):
                        c_s = cum_ref[h, r0 + s:r0 + s + 1, :]
                        k_s = k_ref[h, r0 + s:r0 + s + 1, :]
                        if s < half:
                            d_top = c_top - c_s
                            if s > 0:
                                d_top = jnp.where(sub_iota >= s, d_top, -jnp.inf)
                            p_top.append(jnp.exp2(d_top) * q_top * k_s)
                            p_bot.append(jnp.exp2(c_bot - c_s) * q_bot * k_s)
                        else:
                            d_bot = c_bot - c_s
                            if s > half:
                                d_bot = jnp.where(sub_iota >= s - half, d_bot, -jnp.inf)
                            p_bot.append(jnp.exp2(d_bot) * q_bot * k_s)
                    tops.append(jnp.concatenate(p_top, axis=-1))
                    bots.append(jnp.concatenate(p_bot, axis=-1))
                prow = (h * nb + b0) * half
                ptop_ref[prow:prow + 2 * half, :] = jnp.concatenate(tops, axis=0).astype(BF16)
                pbot_ref[prow:prow + 2 * half, :] = jnp.concatenate(bots, axis=0).astype(BF16)
        grows = slice(h0 * nb * half, (h0 + group) * nb * half)
        a_top = _dot(ptop_ref[grows, :], sel_ref[0:half * dh, :])
        a_bot = _dot(pbot_ref[grows, :], sel_ref[...])

        for h in range(h0, h0 + group):
            cols = head_cols[h]
            q3 = q_ref[h].reshape(nb, sub, dh)
            k3 = k_ref[h].reshape(nb, sub, dh)
            c3 = cum_ref[h].reshape(nb, sub, dh)
            last = c3[:, sub - 1:sub, :]
            k_side = (k3 * jnp.exp2(last - c3)).reshape(ts, dh).astype(BF16)
            q_side = []
            for delta in range(1, per):
                ref_c = jnp.concatenate([jnp.zeros((delta, 1, dh), F32), last[:nb - delta]], axis=0)
                expo = jnp.where(blk >= delta, c3 - ref_c, 0.0)
                q_side.append((q3 * jnp.exp2(expo)).reshape(ts, dh))
            cross = _dot_nt(jnp.concatenate(q_side, axis=0).astype(BF16), k_side)

            rows = slice((h - h0) * nb * half, (h - h0 + 1) * nb * half)
            within = jnp.concatenate([a_top[rows].reshape(nb, half, ts), a_bot[rows].reshape(nb, half, ts)],
                                     axis=1).reshape(ts, ts)
            scores = jnp.where(dist_is[0], within, 0.0)
            for delta in range(1, per):
                scores = jnp.where(dist_is[delta], cross[(delta - 1) * ts:delta * ts], scores)
            o = o_ref[h] + _dot(scores.astype(BF16), v_ref[h])
            act_ref[:, cols] = (_rms(o, hn_ref[:, cols]) * og_ref[h]).astype(BF16)

    y_ref[...] = x + _dot(act_ref[:, 0:d], wout_ref[...])


def _hgrn_mixer(h, norm_g, w_in, lb_logits, head_norm, w_out, layer, slot):
    b, s, d = h.shape
    ts, dh = HGRN_TILE, HGRN_HEAD_DIM
    assert s % ts == 0 and d % dh == 0 and ts % HGRN_CHUNK == 0 and HGRN_CHUNK % HGRN_SUB == 0
    heads = d // dh
    n_slots = lb_logits.shape[0]
    sel, dist = _hgrn_tables()
    prow = heads * (ts // HGRN_SUB) * (HGRN_SUB // 2)
    tile_spec = pl.BlockSpec((None, ts, d), lambda bi, si: (bi, si, 0))
    return pl.pallas_call(
        functools.partial(_hgrn_kernel, layer_slot=slot),
        grid=(b, s // ts),
        in_specs=[tile_spec, _layer_spec((1, d), layer), _layer_spec((d, 4 * d), slot),
                  _const_spec((n_slots, d)), _layer_spec((1, d), slot), _layer_spec((d, d), slot),
                  _const_spec(sel.shape), _const_spec(dist.shape)],
        out_specs=tile_spec,
        out_shape=jax.ShapeDtypeStruct((b, s, d), F32),
        scratch_shapes=[pltpu.VMEM((heads, ts, dh), F32), pltpu.VMEM((heads, ts, dh), F32),
                        pltpu.VMEM((heads, ts, dh), BF16), pltpu.VMEM((heads, ts, dh), F32),
                        pltpu.VMEM((heads, ts, dh), F32), pltpu.VMEM((heads, ts, dh), F32),
                        pltpu.VMEM((prow, (HGRN_SUB // 2) * dh), BF16), pltpu.VMEM((prow, HGRN_SUB * dh), BF16),
                        pltpu.VMEM((ts, d + LANES), BF16), pltpu.VMEM((heads, dh, dh), F32)],
        compiler_params=_params("parallel", "arbitrary"),
        name="hgrn_mixer",
    )(h, norm_g, w_in, lb_logits, head_norm, w_out, sel, dist)


def _conv_shift_matrix():
    shift = np.zeros((CONV_ROWS, CONV_SHIFT_K), np.float32)
    t = np.arange(CONV_ROWS)
    for c in range(1, SUBLANES):
        shift[t, (c - 1) * CONV_PHASE_ROWS + CONV_UNIT + t - c] = 1.0
    return jnp.asarray(shift, BF16)


def _conv_kernel(x_ref, g_ref, win_ref, bin_ref, dwx_ref, dwb_ref, lng_ref, lnb_ref, wout_ref, bout_ref,
                 shift_ref, y_ref, u_ref, c_ref, v_ref):
    ts, d = x_ref.shape
    halo, sl, unit, depth = CONV_HALO, SUBLANES, CONV_UNIT, CONV_TAP_TILES

    @pl.when(pl.program_id(1) == 0)
    def _():
        u_ref[0:halo, 0:d] = jnp.zeros((halo, d), F32)
        pad = CONV_SHIFT_K - (sl - 1) * CONV_PHASE_ROWS
        v_ref[CONV_SHIFT_K - pad:CONV_SHIFT_K, 0:d] = jnp.zeros((pad, d), BF16)

    x = x_ref[...]
    xn = _rms(x, g_ref[...]).astype(BF16)
    a = _dot(xn, win_ref[:, 0:d]) + bin_ref[:, 0:d]
    gate = _dot(xn, win_ref[:, d:2 * d]) + bin_ref[:, d:2 * d]
    u_ref[halo:halo + ts, 0:d] = a * jax.nn.sigmoid(gate)

    for col in range(d // LANES):
        cs = slice(col * LANES, (col + 1) * LANES)
        bias = jnp.broadcast_to(dwb_ref[:, cs], (sl, LANES))
        for m in range(-1, ts // unit):
            base = halo + m * unit - (depth - 1) * sl
            window = [u_ref[base + j * sl:base + (j + 1) * sl, cs] for j in range(depth + 1)]
            for c in range(sl):
                if m < 0 and c == 0:
                    continue
                top, bot = (bias, bias) if c == 0 else (None, None)
                for a_i in range(depth):
                    r = a_i * sl + c
                    if r >= CONV_WIDTH:
                        continue
                    j = CONV_WIDTH - 1 - r
                    tap = dwx_ref[j * sl:(j + 1) * sl, cs]
                    t_term = tap * window[depth - 1 - a_i]
                    b_term = tap * window[depth - a_i]
                    top = t_term if top is None else top + t_term
                    bot = b_term if bot is None else bot + b_term
                both = jnp.concatenate([top, bot], axis=0)
                if c == 0:
                    c_ref[m * unit:(m + 1) * unit, cs] = both
                else:
                    row0 = (c - 1) * CONV_PHASE_ROWS + (m + 1) * unit
                    v_ref[row0:row0 + unit, cs] = both.astype(BF16)
    u_ref[0:halo, 0:d] = u_ref[ts:ts + halo, 0:d]

    c = c_ref[:, 0:d] + _dot(shift_ref[...], v_ref[:, 0:d])
    mu = jnp.mean(c, axis=-1, keepdims=True)
    cc = c - mu
    var = jnp.mean(cc * cc, axis=-1, keepdims=True)
    z = cc * lax.rsqrt(var + LN_EPS) * lng_ref[...] + lnb_ref[...]
    z = (z * jax.nn.sigmoid(z)).astype(BF16)
    y_ref[...] = x + _dot(z, wout_ref[...]) + bout_ref[...]


def _conv_mixer(h, norm_g, w_in, b_in, dw, dw_b, ln_g, ln_b, w_out, b_out, layer, slot):
    b, s, d = h.shape
    ts = CONV_ROWS
    assert s % ts == 0 and ts % CONV_UNIT == 0 and ts >= CONV_HALO
    dwx = jnp.pad(jnp.repeat(dw[slot], SUBLANES, axis=0), ((0, 0), (0, CONV_PAD_LANES)))
    pitch = d + CONV_PAD_LANES
    shift = _conv_shift_matrix()
    tile_spec = pl.BlockSpec((None, ts, d), lambda bi, si: (bi, si, 0))
    vec = lambda: _layer_spec((1, d), slot)
    return pl.pallas_call(
        _conv_kernel,
        grid=(b, s // ts),
        in_specs=[tile_spec, _layer_spec((1, d), layer), _layer_spec((d, 2 * d), slot),
                  _layer_spec((1, 2 * d), slot), _const_spec((CONV_WIDTH * SUBLANES, pitch)), vec(), vec(),
                  vec(), _layer_spec((d, d), slot), vec(), _const_spec(shift.shape)],
        out_specs=tile_spec,
        out_shape=jax.ShapeDtypeStruct((b, s, d), F32),
        scratch_shapes=[pltpu.VMEM((ts + CONV_HALO, pitch), F32), pltpu.VMEM((ts, pitch), F32),
                        pltpu.VMEM((CONV_SHIFT_K, pitch), BF16)],
        compiler_params=_params("parallel", "arbitrary"),
        name="conv_mixer",
    )(h, norm_g, w_in, b_in, dwx, dw_b, ln_g, ln_b, w_out, b_out, shift)


def _memkv_kernel(m_ref, g_ref, w_ref, kv_ref):
    mn = _rms(m_ref[...], g_ref[...]).astype(BF16)
    kv_ref[...] = _dot(mn, w_ref[...]).astype(BF16)


def _memkv(mem2d, mem_norm, wkv):
    n, d = mem2d.shape
    n_layers = wkv.shape[0]
    rows = min(ROW_TILE, n)
    return pl.pallas_call(
        _memkv_kernel,
        grid=(n_layers, n // rows),
        in_specs=[pl.BlockSpec((rows, d), lambda li, i: (i, 0)), _const_spec((1, d)),
                  pl.BlockSpec((None, d, 2 * d), lambda li, i: (li, 0, 0))],
        out_specs=pl.BlockSpec((None, rows, 2 * d), lambda li, i: (li, i, 0)),
        out_shape=jax.ShapeDtypeStruct((n_layers, n, 2 * d), BF16),
        compiler_params=_params("arbitrary", "arbitrary"),
        name="mem_kv",
    )(mem2d, mem_norm.reshape(1, d), wkv)


def _xattn_kernel(x_ref, g_ref, wq_ref, kv_ref, wo_ref, y_ref, act_ref, *, heads):
    ts, d = x_ref.shape
    dh = d // heads
    scale = dh ** -0.5
    x = x_ref[...]
    xn = _rms(x, g_ref[...]).astype(BF16)
    q = _dot(xn, wq_ref[...]).astype(BF16)
    for h in range(heads):
        cols = slice(h * dh, (h + 1) * dh)
        s = _dot_nt(q[:, cols], kv_ref[:, cols]) * scale
        s = s - jnp.max(s, axis=-1, keepdims=True)
        p = jnp.exp(s)
        p = p / jnp.sum(p, axis=-1, keepdims=True)
        act_ref[:, cols] = _dot(p.astype(BF16), kv_ref[:, d + h * dh:d + (h + 1) * dh]).astype(BF16)
    y_ref[...] = x + _dot(act_ref[...], wo_ref[...])


def _xattn(h, norm_g, wq, kv, wo, layer, heads):
    b, s, d = h.shape
    m = kv.shape[2]
    ts = XATTN_ROWS
    tile_spec = pl.BlockSpec((None, ts, d), lambda bi, si: (bi, si, 0))
    return pl.pallas_call(
        functools.partial(_xattn_kernel, heads=heads),
        grid=(b, s // ts),
        in_specs=[tile_spec, _layer_spec((1, d), layer), _layer_spec((d, d), layer),
                  pl.BlockSpec((None, None, m, 2 * d), lambda bi, si: (layer, bi, 0, 0)),
                  _layer_spec((d, d), layer)],
        out_specs=tile_spec,
        out_shape=jax.ShapeDtypeStruct((b, s, d), F32),
        scratch_shapes=[pltpu.VMEM((ts, d), BF16)],
        compiler_params=_params("parallel", "parallel"),
        name="xattn",
    )(h, norm_g, wq, kv, wo)


def kernel(x, mem, ffn1_norm, ffn1_w_in, ffn1_w_out, mix_norm, hgrn_w_in, hgrn_head_norm, hgrn_w_out,
           hgrn_lb_logits, conv_w_in, conv_b_in, conv_dw, conv_dw_b, conv_ln_g, conv_ln_b, conv_w_out,
           conv_b_out, xattn_norm, xattn_wq, xattn_wkv, xattn_wo, ffn2_norm, ffn2_w_in, ffn2_w_out,
           mem_norm, final_norm):
    b, s, d = x.shape
    m = mem.shape[1]
    depth = ffn1_norm.shape[0]
    t = b * s
    bf = lambda w: w.astype(BF16)
    vec = lambda a: a.reshape(a.shape[0], 1, a.shape[1])
    flat = lambda a: a.reshape(t, d)
    cube = lambda a: a.reshape(b, s, d)

    ffn1_w_in, ffn1_w_out, ffn2_w_in, ffn2_w_out = bf(ffn1_w_in), bf(ffn1_w_out), bf(ffn2_w_in), bf(ffn2_w_out)
    hgrn_w_in, hgrn_w_out = bf(hgrn_w_in), bf(hgrn_w_out)
    conv_w_in, conv_w_out = bf(conv_w_in), bf(conv_w_out)
    xattn_wq, xattn_wo = bf(xattn_wq), bf(xattn_wo)
    ffn1_norm, ffn2_norm, mix_norm, xattn_norm = vec(ffn1_norm), vec(ffn2_norm), vec(mix_norm), vec(xattn_norm)

    kv = _memkv(mem.reshape(b * m, d), mem_norm, bf(xattn_wkv)).reshape(depth, b, m, 2 * d)
    h = x
    for layer in range(depth):
        h = cube(_ffn(flat(h), ffn1_norm, ffn1_w_in, ffn1_w_out, layer))
        j = layer // 2
        if layer % 2 == 0:
            h = _hgrn_mixer(h, mix_norm, hgrn_w_in, hgrn_lb_logits, vec(hgrn_head_norm), hgrn_w_out, layer, j)
        else:
            h = _conv_mixer(h, mix_norm, conv_w_in, vec(conv_b_in), conv_dw, vec(conv_dw_b), vec(conv_ln_g),
                            vec(conv_ln_b), conv_w_out, vec(conv_b_out), layer, j)
        h = _xattn(h, xattn_norm, xattn_wq, kv, xattn_wo, layer, XATTN_HEADS)
        last = layer == depth - 1
        h = cube(_ffn(flat(h), ffn2_norm, ffn2_w_in, ffn2_w_out, layer, final_norm if last else None))
    return h
```

```python
import functools

import numpy as np
import jax
import jax.numpy as jnp
from jax import lax
from jax.experimental import pallas as pl
from jax.experimental.pallas import tpu as pltpu

F32 = jnp.float32
BF16 = jnp.bfloat16

RMS_EPS = 1e-6
LN_EPS = 1e-5

HGRN_HEAD_DIM = 128
XATTN_HEADS = 4
CONV_WIDTH = 31

LANES = 128
SUBLANES = 8
VMEM_LIMIT_BYTES = 56 * 1024 * 1024

HGRN_TILE = 2 * LANES
HGRN_CHUNK = 64
HGRN_SUB = 2 * SUBLANES

FFN_ROWS = 1024
FFN_COLS = 256
ROW_TILE = 512
XATTN_ROWS = 1024
CONV_ROWS = 256
CONV_UNIT = 2 * SUBLANES
CONV_TAP_TILES = -(-CONV_WIDTH // SUBLANES)
CONV_HALO = CONV_UNIT + (CONV_TAP_TILES - 1) * SUBLANES
CONV_PHASE_ROWS = CONV_ROWS + CONV_UNIT
CONV_SHIFT_K = -(-((SUBLANES - 1) * CONV_PHASE_ROWS) // LANES) * LANES
CONV_PAD_LANES = LANES
CONV_COL_CHUNK = 2 * LANES


def _const_spec(shape):
    zeros = (0,) * len(shape)
    return pl.BlockSpec(shape, lambda *_: zeros, pipeline_mode=pl.Buffered(1))


def _layer_spec(shape, layer):
    zeros = (0,) * len(shape)
    return pl.BlockSpec((None,) + tuple(shape), lambda *_: (layer,) + zeros,
                        pipeline_mode=pl.Buffered(1))


def _params(*semantics):
    return pltpu.CompilerParams(dimension_semantics=semantics,
                                vmem_limit_bytes=VMEM_LIMIT_BYTES)


def _rms(x, g):
    ms = jnp.mean(x * x, axis=-1, keepdims=True)
    return x * lax.rsqrt(ms + RMS_EPS) * g


def _dot(a, b):
    return jnp.dot(a, b, preferred_element_type=F32)


def _dot_nt(a, b):
    return lax.dot_general(a, b, (((1,), (1,)), ((), ())), preferred_element_type=F32)


def _dot_tn(a, b):
    return lax.dot_general(a, b, (((0,), (0,)), ((), ())), preferred_element_type=F32)


def _ffn_kernel(*refs, d_ff, final):
    if final:
        x_ref, g_ref, win_ref, wout_ref, fg_ref, o_ref, act_ref = refs
    else:
        x_ref, g_ref, win_ref, wout_ref, o_ref, act_ref = refs
    x = x_ref[...]
    xn = _rms(x, g_ref[...]).astype(BF16)
    for c in range(d_ff // FFN_COLS):
        lo = c * FFN_COLS
        gate = _dot(xn, win_ref[:, lo:lo + FFN_COLS])
        up = _dot(xn, win_ref[:, d_ff + lo:d_ff + lo + FFN_COLS])
        act_ref[:, lo:lo + FFN_COLS] = (gate * jax.nn.sigmoid(gate) * up).astype(BF16)
    y = x + 0.5 * _dot(act_ref[...], wout_ref[...])
    if final:
        y = _rms(y, fg_ref[...])
    o_ref[...] = y


def _ffn(h, norm_g, w_in, w_out, layer, final_g=None):
    t, d = h.shape
    d_ff = w_out.shape[1]
    assert t % FFN_ROWS == 0 and d_ff % FFN_COLS == 0
    final = final_g is not None
    row_spec = pl.BlockSpec((FFN_ROWS, d), lambda i: (i, 0))
    in_specs = [row_spec, _layer_spec((1, d), layer), _layer_spec((d, 2 * d_ff), layer),
                _layer_spec((d_ff, d), layer)]
    args = [h, norm_g, w_in, w_out]
    if final:
        in_specs.append(_const_spec((1, d)))
        args.append(final_g.reshape(1, d))
    return pl.pallas_call(
        functools.partial(_ffn_kernel, d_ff=d_ff, final=final),
        grid=(t // FFN_ROWS,),
        in_specs=in_specs,
        out_specs=row_spec,
        out_shape=jax.ShapeDtypeStruct((t, d), F32),
        scratch_shapes=[pltpu.VMEM((FFN_ROWS, d_ff), BF16)],
        compiler_params=_params("parallel"),
        name="ffn",
    )(*args)


def _hgrn_tables():
    sel = np.zeros((HGRN_SUB * LANES, HGRN_TILE), np.float32)
    for s in range(HGRN_SUB):
        sel[s * LANES:(s + 1) * LANES, s::HGRN_SUB] = 1.0
    r = np.arange(HGRN_TILE)
    same_chunk = (r[:, None] // HGRN_CHUNK) == (r[None, :] // HGRN_CHUNK)
    dist = r[:, None] // HGRN_SUB - r[None, :] // HGRN_SUB
    dist = np.where(same_chunk & (dist >= 0), dist, -1).astype(np.int32)
    return jnp.asarray(sel, BF16), jnp.asarray(dist)


def _hgrn_kernel(x_ref, g_ref, win_ref, lbl_ref, hn_ref, wout_ref, sel_ref, dist_ref, y_ref,
                 q_ref, k_ref, v_ref, cum_ref, og_ref, o_ref, ptop_ref, pbot_ref, act_ref, st_ref, *, layer_slot):
    ts, d = x_ref.shape
    dh, sub, chunk = HGRN_HEAD_DIM, HGRN_SUB, HGRN_CHUNK
    heads, nb, nc, per = d // dh, ts // sub, ts // chunk, chunk // sub
    half = sub // 2
    head_cols = [slice(h * dh, (h + 1) * dh) for h in range(heads)]

    @pl.when(pl.program_id(1) == 0)
    def _():
        st_ref[...] = jnp.zeros_like(st_ref)

    x = x_ref[...]
    xn = _rms(x, g_ref[...]).astype(BF16)
    logits = lbl_ref[...]
    e = jnp.exp(logits - jnp.max(logits, axis=0, keepdims=True))
    sm = e / jnp.sum(e, axis=0, keepdims=True)
    lb = jnp.sum(sm[0:layer_slot + 1], axis=0, keepdims=True)

    qp = _dot(xn, win_ref[:, 0:d])
    qs = qp * jax.nn.sigmoid(qp)
    fp = _dot(xn, win_ref[:, d:2 * d])
    ks = (1.0 - lb) * jax.nn.sigmoid(-fp)
    cum = jnp.log2(lb + (1.0 - lb) * jax.nn.sigmoid(fp))
    row = lax.broadcasted_iota(jnp.int32, (ts, d), 0) % chunk
    sh = 1
    while sh < chunk:
        cum = cum + jnp.where(row >= sh, pltpu.roll(cum, sh, axis=0), 0.0)
        sh *= 2
    for h, cols in enumerate(head_cols):
        q_ref[h] = qs[:, cols]
        k_ref[h] = ks[:, cols]
        cum_ref[h] = cum[:, cols]
    vp = _dot(xn, win_ref[:, 2 * d:3 * d]).astype(BF16)
    gp = _dot(xn, win_ref[:, 3 * d:4 * d])
    gs = gp * jax.nn.sigmoid(gp)
    for h, cols in enumerate(head_cols):
        v_ref[h] = vp[:, cols]
        og_ref[h] = gs[:, cols]

    q_dec, k_dec, c_end = [], [], []
    for h in range(heads):
        cum_h = cum_ref[h]
        c4 = cum_h.reshape(nc, chunk, dh)
        end = c4[:, chunk - 1:chunk, :]
        c_end.append(end)
        q_dec.append((q_ref[h] * jnp.exp2(cum_h)).astype(BF16))
        k_dec.append((k_ref[h].reshape(nc, chunk, dh) * jnp.exp2(end - c4)).astype(BF16))
    states = [st_ref[h] for h in range(heads)]
    for c in range(nc):
        crow = slice(c * chunk, (c + 1) * chunk)
        for h, cols in enumerate(head_cols):
            o_ref[h, crow, :] = _dot_nt(q_dec[h][crow], states[h].astype(BF16))
            states[h] = states[h] * jnp.exp2(c_end[h][c]) + _dot_tn(v_ref[h, crow, :], k_dec[h][c])
    for h in range(heads):
        st_ref[h] = states[h]

    dist = dist_ref[...]
    dist_is = [dist == delta for delta in range(per)]
    blk = lax.broadcasted_iota(jnp.int32, (nb, 1, dh), 0) % per
    sub_iota = lax.broadcasted_iota(jnp.int32, (half, dh), 0)
    group = 2
    for h0 in range(0, heads, group):
        for h in range(h0, h0 + group):
            cols = head_cols[h]
            for b0 in range(0, nb, 2):
                tops, bots = [], []
                for r0 in (b0 * sub, (b0 + 1) * sub):
                    q_top, q_bot = q_ref[h, r0:r0 + half, :], q_ref[h, r0 + half:r0 + sub, :]
                    c_top, c_bot = cum_ref[h, r0:r0 + half, :], cum_ref[h, r0 + half:r0 + sub, :]
                    p_top, p_bot = [], []
                    for s in range(sub):
                        c_s = cum_ref[h, r0 + s:r0 + s + 1, :]
                        k_s = k_ref[h, r0 + s:r0 + s + 1, :]
                        if s < half:
                            d_top = c_top - c_s
                            if s > 0:
                                d_top = jnp.where(sub_iota >= s, d_top, -jnp.inf)
                            p_top.append(jnp.exp2(d_top) * q_top * k_s)
                            p_bot.append(jnp.exp2(c_bot - c_s) * q_bot * k_s)
                        else:
                            d_bot = c_bot - c_s
                            if s > half:
                                d_bot = jnp.where(sub_iota >= s - half, d_bot, -jnp.inf)
                            p_bot.append(jnp.exp2(d_bot) * q_bot * k_s)
                    tops.append(jnp.concatenate(p_top, axis=-1))
                    bots.append(jnp.concatenate(p_bot, axis=-1))
                prow = (h * nb + b0) * half
                ptop_ref[prow:prow + 2 * half, :] = jnp.concatenate(tops, axis=0).astype(BF16)
                pbot_ref[prow:prow + 2 * half, :] = jnp.concatenate(bots, axis=0).astype(BF16)
        grows = slice(h0 * nb * half, (h0 + group) * nb * half)
        a_top = _dot(ptop_ref[grows, :], sel_ref[0:half * dh, :])
        a_bot = _dot(pbot_ref[grows, :], sel_ref[...])

        for h in range(h0, h0 + group):
            cols = head_cols[h]
            q3 = q_ref[h].reshape(nb, sub, dh)
            k3 = k_ref[h].reshape(nb, sub, dh)
            c3 = cum_ref[h].reshape(nb, sub, dh)
            last = c3[:, sub - 1:sub, :]
            k_side = (k3 * jnp.exp2(last - c3)).reshape(ts, dh).astype(BF16)
            q_side = []
            for delta in range(1, per):
                ref_c = jnp.concatenate([jnp.zeros((delta, 1, dh), F32), last[:nb - delta]], axis=0)
                expo = jnp.where(blk >= delta, c3 - ref_c, 0.0)
                q_side.append((q3 * jnp.exp2(expo)).reshape(ts, dh))
            cross = _dot_nt(jnp.concatenate(q_side, axis=0).astype(BF16), k_side)

            rows = slice((h - h0) * nb * half, (h - h0 + 1) * nb * half)
            within = jnp.concatenate([a_top[rows].reshape(nb, half, ts), a_bot[rows].reshape(nb, half, ts)],
                                     axis=1).reshape(ts, ts)
            scores = jnp.where(dist_is[0], within, 0.0)
            for delta in range(1, per):
                scores = jnp.where(dist_is[delta], cross[(delta - 1) * ts:delta * ts], scores)
            o = o_ref[h] + _dot(scores.astype(BF16), v_ref[h])
            act_ref[:, cols] = (_rms(o, hn_ref[:, cols]) * og_ref[h]).astype(BF16)

    y_ref[...] = x + _dot(act_ref[:, 0:d], wout_ref[...])


def _hgrn_mixer(h, norm_g, w_in, lb_logits, head_norm, w_out, layer, slot):
    b, s, d = h.shape
    ts, dh = HGRN_TILE, HGRN_HEAD_DIM
    assert s % ts == 0 and d % dh == 0 and ts % HGRN_CHUNK == 0 and HGRN_CHUNK % HGRN_SUB == 0
    heads = d // dh
    n_slots = lb_logits.shape[0]
    sel, dist = _hgrn_tables()
    prow = heads * (ts // HGRN_SUB) * (HGRN_SUB // 2)
    tile_spec = pl.BlockSpec((None, ts, d), lambda bi, si: (bi, si, 0))
    return pl.pallas_call(
        functools.partial(_hgrn_kernel, layer_slot=slot),
        grid=(b, s // ts),
        in_specs=[tile_spec, _layer_spec((1, d), layer), _layer_spec((d, 4 * d), slot),
                  _const_spec((n_slots, d)), _layer_spec((1, d), slot), _layer_spec((d, d), slot),
                  _const_spec(sel.shape), _const_spec(dist.shape)],
        out_specs=tile_spec,
        out_shape=jax.ShapeDtypeStruct((b, s, d), F32),
        scratch_shapes=[pltpu.VMEM((heads, ts, dh), F32), pltpu.VMEM((heads, ts, dh), F32),
                        pltpu.VMEM((heads, ts, dh), BF16), pltpu.VMEM((heads, ts, dh), F32),
                        pltpu.VMEM((heads, ts, dh), F32), pltpu.VMEM((heads, ts, dh), F32),
                        pltpu.VMEM((prow, (HGRN_SUB // 2) * dh), BF16), pltpu.VMEM((prow, HGRN_SUB * dh), BF16),
                        pltpu.VMEM((ts, d + LANES), BF16), pltpu.VMEM((heads, dh, dh), F32)],
        compiler_params=_params("parallel", "arbitrary"),
        name="hgrn_mixer",
    )(h, norm_g, w_in, lb_logits, head_norm, w_out, sel, dist)


def _conv_shift_matrix():
    shift = np.zeros((CONV_ROWS, CONV_SHIFT_K), np.float32)
    t = np.arange(CONV_ROWS)
    for c in range(1, SUBLANES):
        shift[t, (c - 1) * CONV_PHASE_ROWS + CONV_UNIT + t - c] = 1.0
    return jnp.asarray(shift, BF16)


def _conv_kernel(x_ref, g_ref, win_ref, bin_ref, dwx_ref, dwb_ref, lng_ref, lnb_ref, wout_ref, bout_ref,
                 shift_ref, y_ref, u_ref, c_ref, v_ref):
    ts, d = x_ref.shape
    halo, sl, unit, depth = CONV_HALO, SUBLANES, CONV_UNIT, CONV_TAP_TILES

    @pl.when(pl.program_id(1) == 0)
    def _():
        u_ref[0:halo, 0:d] = jnp.zeros((halo, d), F32)
        pad = CONV_SHIFT_K - (sl - 1) * CONV_PHASE_ROWS
        v_ref[CONV_SHIFT_K - pad:CONV_SHIFT_K, 0:d] = jnp.zeros((pad, d), BF16)

    x = x_ref[...]
    xn = _rms(x, g_ref[...]).astype(BF16)
    chunk = CONV_COL_CHUNK
    for c0 in range(0, d, chunk):
        cc_ = slice(c0, c0 + chunk)
        a = _dot(xn, win_ref[:, c0:c0 + chunk]) + bin_ref[:, c0:c0 + chunk]
        gate = _dot(xn, win_ref[:, d + c0:d + c0 + chunk]) + bin_ref[:, d + c0:d + c0 + chunk]
        u_ref[halo:halo + ts, cc_] = a * jax.nn.sigmoid(gate)
        for col in range(c0 // LANES, (c0 + chunk) // LANES):
            cs = slice(col * LANES, (col + 1) * LANES)
            bias = jnp.broadcast_to(dwb_ref[:, cs], (sl, LANES))
            for m in range(-1, ts // unit):
                base = halo + m * unit - (depth - 1) * sl
                window = [u_ref[base + j * sl:base + (j + 1) * sl, cs] for j in range(depth + 1)]
                for c in range(sl):
                    if m < 0 and c == 0:
                        continue
                    top, bot = (bias, bias) if c == 0 else (None, None)
                    for a_i in range(depth):
                        r = a_i * sl + c
                        if r >= CONV_WIDTH:
                            continue
                        j = CONV_WIDTH - 1 - r
                        tap = dwx_ref[j * sl:(j + 1) * sl, cs]
                        t_term = tap * window[depth - 1 - a_i]
                        b_term = tap * window[depth - a_i]
                        top = t_term if top is None else top + t_term
                        bot = b_term if bot is None else bot + b_term
                    both = jnp.concatenate([top, bot], axis=0)
                    if c == 0:
                        c_ref[m * unit:(m + 1) * unit, cs] = both
                    else:
                        row0 = (c - 1) * CONV_PHASE_ROWS + (m + 1) * unit
                        v_ref[row0:row0 + unit, cs] = both.astype(BF16)
        u_ref[0:halo, cc_] = u_ref[ts:ts + halo, cc_]
        c_ref[:, cc_] = c_ref[:, cc_] + _dot(shift_ref[...], v_ref[:, cc_])

    c = c_ref[:, 0:d]
    mu = jnp.mean(c, axis=-1, keepdims=True)
    cc = c - mu
    var = jnp.mean(cc * cc, axis=-1, keepdims=True)
    z = cc * lax.rsqrt(var + LN_EPS) * lng_ref[...] + lnb_ref[...]
    z = (z * jax.nn.sigmoid(z)).astype(BF16)
    y_ref[...] = x + _dot(z, wout_ref[...]) + bout_ref[...]


def _conv_mixer(h, norm_g, w_in, b_in, dw, dw_b, ln_g, ln_b, w_out, b_out, layer, slot):
    b, s, d = h.shape
    ts = CONV_ROWS
    assert s % ts == 0 and ts % CONV_UNIT == 0 and ts >= CONV_HALO
    dwx = jnp.pad(jnp.repeat(dw[slot], SUBLANES, axis=0), ((0, 0), (0, CONV_PAD_LANES)))
    pitch = d + CONV_PAD_LANES
    shift = _conv_shift_matrix()
    tile_spec = pl.BlockSpec((None, ts, d), lambda bi, si: (bi, si, 0))
    vec = lambda: _layer_spec((1, d), slot)
    return pl.pallas_call(
        _conv_kernel,
        grid=(b, s // ts),
        in_specs=[tile_spec, _layer_spec((1, d), layer), _layer_spec((d, 2 * d), slot),
                  _layer_spec((1, 2 * d), slot), _const_spec((CONV_WIDTH * SUBLANES, pitch)), vec(), vec(),
                  vec(), _layer_spec((d, d), slot), vec(), _const_spec(shift.shape)],
        out_specs=tile_spec,
        out_shape=jax.ShapeDtypeStruct((b, s, d), F32),
        scratch_shapes=[pltpu.VMEM((ts + CONV_HALO, pitch), F32), pltpu.VMEM((ts, pitch), F32),
                        pltpu.VMEM((CONV_SHIFT_K, pitch), BF16)],
        compiler_params=_params("parallel", "arbitrary"),
        name="conv_mixer",
    )(h, norm_g, w_in, b_in, dwx, dw_b, ln_g, ln_b, w_out, b_out, shift)


def _memkv_kernel(m_ref, g_ref, w_ref, kv_ref):
    mn = _rms(m_ref[...], g_ref[...]).astype(BF16)
    kv_ref[...] = _dot(mn, w_ref[...]).astype(BF16)


def _memkv(mem2d, mem_norm, wkv):
    n, d = mem2d.shape
    n_layers = wkv.shape[0]
    rows = min(ROW_TILE, n)
    return pl.pallas_call(
        _memkv_kernel,
        grid=(n_layers, n // rows),
        in_specs=[pl.BlockSpec((rows, d), lambda li, i: (i, 0)), _const_spec((1, d)),
                  pl.BlockSpec((None, d, 2 * d), lambda li, i: (li, 0, 0))],
        out_specs=pl.BlockSpec((None, rows, 2 * d), lambda li, i: (li, i, 0)),
        out_shape=jax.ShapeDtypeStruct((n_layers, n, 2 * d), BF16),
        compiler_params=_params("arbitrary", "arbitrary"),
        name="mem_kv",
    )(mem2d, mem_norm.reshape(1, d), wkv)


def _xattn_kernel(x_ref, g_ref, wq_ref, kv_ref, wo_ref, y_ref, act_ref, *, heads):
    ts, d = x_ref.shape
    dh = d // heads
    scale = dh ** -0.5
    x = x_ref[...]
    xn = _rms(x, g_ref[...]).astype(BF16)
    q = _dot(xn, wq_ref[...]).astype(BF16)
    for h in range(heads):
        cols = slice(h * dh, (h + 1) * dh)
        s = _dot_nt(q[:, cols], kv_ref[:, cols]) * scale
        s = s - jnp.max(s, axis=-1, keepdims=True)
        p = jnp.exp(s)
        p = p / jnp.sum(p, axis=-1, keepdims=True)
        act_ref[:, cols] = _dot(p.astype(BF16), kv_ref[:, d + h * dh:d + (h + 1) * dh]).astype(BF16)
    y_ref[...] = x + _dot(act_ref[...], wo_ref[...])


def _xattn(h, norm_g, wq, kv, wo, layer, heads):
    b, s, d = h.shape
    m = kv.shape[2]
    ts = XATTN_ROWS
    tile_spec = pl.BlockSpec((None, ts, d), lambda bi, si: (bi, si, 0))
    return pl.pallas_call(
        functools.partial(_xattn_kernel, heads=heads),
        grid=(b, s // ts),
        in_specs=[tile_spec, _layer_spec((1, d), layer), _layer_spec((d, d), layer),
                  pl.BlockSpec((None, None, m, 2 * d), lambda bi, si: (layer, bi, 0, 0)),
                  _layer_spec((d, d), layer)],
        out_specs=tile_spec,
        out_shape=jax.ShapeDtypeStruct((b, s, d), F32),
        scratch_shapes=[pltpu.VMEM((ts, d), BF16)],
        compiler_params=_params("parallel", "parallel"),
        name="xattn",
    )(h, norm_g, wq, kv, wo)


def kernel(x, mem, ffn1_norm, ffn1_w_in, ffn1_w_out, mix_norm, hgrn_w_in, hgrn_head_norm, hgrn_w_out,
           hgrn_lb_logits, conv_w_in, conv_b_in, conv_dw, conv_dw_b, conv_ln_g, conv_ln_b, conv_w_out,
           conv_b_out, xattn_norm, xattn_wq, xattn_wkv, xattn_wo, ffn2_norm, ffn2_w_in, ffn2_w_out,
           mem_norm, final_norm):
    b, s, d = x.shape
    m = mem.shape[1]
    depth = ffn1_norm.shape[0]
    t = b * s
    bf = lambda w: w.astype(BF16)
    vec = lambda a: a.reshape(a.shape[0], 1, a.shape[1])
    flat = lambda a: a.reshape(t, d)
    cube = lambda a: a.reshape(b, s, d)

    ffn1_w_in, ffn1_w_out, ffn2_w_in, ffn2_w_out = bf(ffn1_w_in), bf(ffn1_w_out), bf(ffn2_w_in), bf(ffn2_w_out)
    hgrn_w_in, hgrn_w_out = bf(hgrn_w_in), bf(hgrn_w_out)
    conv_w_in, conv_w_out = bf(conv_w_in), bf(conv_w_out)
    xattn_wq, xattn_wo = bf(xattn_wq), bf(xattn_wo)
    ffn1_norm, ffn2_norm, mix_norm, xattn_norm = vec(ffn1_norm), vec(ffn2_norm), vec(mix_norm), vec(xattn_norm)

    kv = _memkv(mem.reshape(b * m, d), mem_norm, bf(xattn_wkv)).reshape(depth, b, m, 2 * d)
    h = x
    for layer in range(depth):
        h = cube(_ffn(flat(h), ffn1_norm, ffn1_w_in, ffn1_w_out, layer))
        j = layer // 2
        if layer % 2 == 0:
            h = _hgrn_mixer(h, mix_norm, hgrn_w_in, hgrn_lb_logits, vec(hgrn_head_norm), hgrn_w_out, layer, j)
        else:
            h = _conv_mixer(h, mix_norm, conv_w_in, vec(conv_b_in), conv_dw, vec(conv_dw_b), vec(conv_ln_g),
                            vec(conv_ln_b), conv_w_out, vec(conv_b_out), layer, j)
        h = _xattn(h, xattn_norm, xattn_wq, kv, xattn_wo, layer, XATTN_HEADS)
        last = layer == depth - 1
        h = cube(_ffn(flat(h), ffn2_norm, ffn2_w_in, ffn2_w_out, layer, final_norm if last else None))
    return h
```

```python
import functools

import numpy as np
import jax
import jax.numpy as jnp
from jax import lax
from jax.experimental import pallas as pl
from jax.experimental.pallas import tpu as pltpu

F32 = jnp.float32
BF16 = jnp.bfloat16

RMS_EPS = 1e-6
LN_EPS = 1e-5

HGRN_HEAD_DIM = 128
XATTN_HEADS = 4
CONV_WIDTH = 31

LANES = 128
SUBLANES = 8
VMEM_LIMIT_BYTES = 56 * 1024 * 1024

HGRN_TILE = 2 * LANES
HGRN_CHUNK = 64
HGRN_SUB = 2 * SUBLANES

FFN_ROWS = 1024
FFN_COLS = 256
ROW_TILE = 512
XATTN_ROWS = 1024
CONV_ROWS = 256
CONV_UNIT = 2 * SUBLANES
CONV_TAP_TILES = -(-CONV_WIDTH // SUBLANES)
CONV_HALO = CONV_UNIT + (CONV_TAP_TILES - 1) * SUBLANES
CONV_PHASE_ROWS = CONV_ROWS + CONV_UNIT
CONV_SHIFT_K = -(-((SUBLANES - 1) * CONV_PHASE_ROWS) // LANES) * LANES
CONV_PAD_LANES = LANES
CONV_COL_CHUNK = 2 * LANES


def _const_spec(shape):
    zeros = (0,) * len(shape)
    return pl.BlockSpec(shape, lambda *_: zeros, pipeline_mode=pl.Buffered(1))


def _layer_spec(shape, layer):
    zeros = (0,) * len(shape)
    return pl.BlockSpec((None,) + tuple(shape), lambda *_: (layer,) + zeros,
                        pipeline_mode=pl.Buffered(1))


def _params(*semantics):
    return pltpu.CompilerParams(dimension_semantics=semantics,
                                vmem_limit_bytes=VMEM_LIMIT_BYTES)


def _rms(x, g):
    ms = jnp.mean(x * x, axis=-1, keepdims=True)
    return x * lax.rsqrt(ms + RMS_EPS) * g


def _dot(a, b):
    return jnp.dot(a, b, preferred_element_type=F32)


def _dot_nt(a, b):
    return lax.dot_general(a, b, (((1,), (1,)), ((), ())), preferred_element_type=F32)


def _dot_tn(a, b):
    return lax.dot_general(a, b, (((0,), (0,)), ((), ())), preferred_element_type=F32)


def _ffn_kernel(*refs, d_ff, final):
    if final:
        x_ref, g_ref, win_ref, wout_ref, fg_ref, o_ref, act_ref = refs
    else:
        x_ref, g_ref, win_ref, wout_ref, o_ref, act_ref = refs
    x = x_ref[...]
    xn = _rms(x, g_ref[...]).astype(BF16)
    for c in range(d_ff // FFN_COLS):
        lo = c * FFN_COLS
        gate = _dot(xn, win_ref[:, lo:lo + FFN_COLS])
        up = _dot(xn, win_ref[:, d_ff + lo:d_ff + lo + FFN_COLS])
        act_ref[:, lo:lo + FFN_COLS] = (gate * jax.nn.sigmoid(gate) * up).astype(BF16)
    y = x + 0.5 * _dot(act_ref[...], wout_ref[...])
    if final:
        y = _rms(y, fg_ref[...])
    o_ref[...] = y


def _ffn(h, norm_g, w_in, w_out, layer, final_g=None):
    t, d = h.shape
    d_ff = w_out.shape[1]
    assert t % FFN_ROWS == 0 and d_ff % FFN_COLS == 0
    final = final_g is not None
    row_spec = pl.BlockSpec((FFN_ROWS, d), lambda i: (i, 0))
    in_specs = [row_spec, _layer_spec((1, d), layer), _layer_spec((d, 2 * d_ff), layer),
                _layer_spec((d_ff, d), layer)]
    args = [h, norm_g, w_in, w_out]
    if final:
        in_specs.append(_const_spec((1, d)))
        args.append(final_g.reshape(1, d))
    return pl.pallas_call(
        functools.partial(_ffn_kernel, d_ff=d_ff, final=final),
        grid=(t // FFN_ROWS,),
        in_specs=in_specs,
        out_specs=row_spec,
        out_shape=jax.ShapeDtypeStruct((t, d), F32),
        scratch_shapes=[pltpu.VMEM((FFN_ROWS, d_ff), BF16)],
        compiler_params=_params("parallel"),
        name="ffn",
    )(*args)


def _hgrn_tables():
    sel = np.zeros((HGRN_SUB * LANES, HGRN_TILE), np.float32)
    for s in range(HGRN_SUB):
        sel[s * LANES:(s + 1) * LANES, s::HGRN_SUB] = 1.0
    r = np.arange(HGRN_TILE)
    same_chunk = (r[:, None] // HGRN_CHUNK) == (r[None, :] // HGRN_CHUNK)
    dist = r[:, None] // HGRN_SUB - r[None, :] // HGRN_SUB
    dist = np.where(same_chunk & (dist >= 0), dist, -1).astype(np.int32)
    return jnp.asarray(sel, BF16), jnp.asarray(dist)


def _hgrn_kernel(x_ref, g_ref, win_ref, lbl_ref, hn_ref, wout_ref, sel_ref, dist_ref, y_ref,
                 q_ref, k_ref, v_ref, cum_ref, kb_ref, og_ref, o_ref, ptop_ref, pbot_ref, act_ref, st_ref,
                 *, layer_slot):
    ts, d = x_ref.shape
    dh, sub, chunk = HGRN_HEAD_DIM, HGRN_SUB, HGRN_CHUNK
    heads, nb, nc, per = d // dh, ts // sub, ts // chunk, chunk // sub
    half = sub // 2
    head_cols = [slice(h * dh, (h + 1) * dh) for h in range(heads)]

    @pl.when(pl.program_id(1) == 0)
    def _():
        st_ref[...] = jnp.zeros_like(st_ref)

    x = x_ref[...]
    xn = _rms(x, g_ref[...]).astype(BF16)
    logits = lbl_ref[...]
    e = jnp.exp(logits - jnp.max(logits, axis=0, keepdims=True))
    sm = e / jnp.sum(e, axis=0, keepdims=True)
    lb = jnp.sum(sm[0:layer_slot + 1], axis=0, keepdims=True)

    qp = _dot(xn, win_ref[:, 0:d])
    qs = qp * jax.nn.sigmoid(qp)
    fp = _dot(xn, win_ref[:, d:2 * d])
    ks = (1.0 - lb) * jax.nn.sigmoid(-fp)
    cum = jnp.log2(lb + (1.0 - lb) * jax.nn.sigmoid(fp))
    row = lax.broadcasted_iota(jnp.int32, (ts, d), 0) % chunk
    sh = 1
    while sh < chunk:
        cum = cum + jnp.where(row >= sh, pltpu.roll(cum, sh, axis=0), 0.0)
        sh *= 2
    kb = cum - jnp.where(ks > 0.0, jnp.log2(ks), -jnp.inf)
    for h, cols in enumerate(head_cols):
        q_ref[h] = qs[:, cols]
        k_ref[h] = ks[:, cols]
        cum_ref[h] = cum[:, cols]
        kb_ref[h] = kb[:, cols]
    vp = _dot(xn, win_ref[:, 2 * d:3 * d]).astype(BF16)
    gp = _dot(xn, win_ref[:, 3 * d:4 * d])
    gs = gp * jax.nn.sigmoid(gp)
    for h, cols in enumerate(head_cols):
        v_ref[h] = vp[:, cols]
        og_ref[h] = gs[:, cols]

    q_dec, k_dec, c_end = [], [], []
    for h in range(heads):
        cum_h = cum_ref[h]
        c4 = cum_h.reshape(nc, chunk, dh)
        end = c4[:, chunk - 1:chunk, :]
        c_end.append(end)
        q_dec.append((q_ref[h] * jnp.exp2(cum_h)).astype(BF16))
        k_dec.append((k_ref[h].reshape(nc, chunk, dh) * jnp.exp2(end - c4)).astype(BF16))
    states = [st_ref[h] for h in range(heads)]
    for c in range(nc):
        crow = slice(c * chunk, (c + 1) * chunk)
        for h, cols in enumerate(head_cols):
            o_ref[h, crow, :] = _dot_nt(q_dec[h][crow], states[h].astype(BF16))
            states[h] = states[h] * jnp.exp2(c_end[h][c]) + _dot_tn(v_ref[h, crow, :], k_dec[h][c])
    for h in range(heads):
        st_ref[h] = states[h]

    dist = dist_ref[...]
    dist_is = [dist == delta for delta in range(per)]
    blk = lax.broadcasted_iota(jnp.int32, (nb, 1, dh), 0) % per
    sub_iota = lax.broadcasted_iota(jnp.int32, (half, dh), 0)
    group = 2
    for h0 in range(0, heads, group):
        for h in range(h0, h0 + group):
            cols = head_cols[h]
            for b0 in range(0, nb, 2):
                tops, bots = [], []
                for r0 in (b0 * sub, (b0 + 1) * sub):
                    q_top, q_bot = q_ref[h, r0:r0 + half, :], q_ref[h, r0 + half:r0 + sub, :]
                    c_top, c_bot = cum_ref[h, r0:r0 + half, :], cum_ref[h, r0 + half:r0 + sub, :]
                    p_top, p_bot = [], []
                    for s in range(sub):
                        b_s = kb_ref[h, r0 + s:r0 + s + 1, :]
                        if s < half:
                            d_top = c_top - b_s
                            if s > 0:
                                d_top = jnp.where(sub_iota >= s, d_top, -jnp.inf)
                            p_top.append(jnp.exp2(d_top) * q_top)
                            p_bot.append(jnp.exp2(c_bot - b_s) * q_bot)
                        else:
                            d_bot = c_bot - b_s
                            if s > half:
                                d_bot = jnp.where(sub_iota >= s - half, d_bot, -jnp.inf)
                            p_bot.append(jnp.exp2(d_bot) * q_bot)
                    tops.append(jnp.concatenate(p_top, axis=-1))
                    bots.append(jnp.concatenate(p_bot, axis=-1))
                prow = (h * nb + b0) * half
                ptop_ref[prow:prow + 2 * half, :] = jnp.concatenate(tops, axis=0).astype(BF16)
                pbot_ref[prow:prow + 2 * half, :] = jnp.concatenate(bots, axis=0).astype(BF16)
        grows = slice(h0 * nb * half, (h0 + group) * nb * half)
        a_top = _dot(ptop_ref[grows, :], sel_ref[0:half * dh, :])
        a_bot = _dot(pbot_ref[grows, :], sel_ref[...])

        for h in range(h0, h0 + group):
            cols = head_cols[h]
            q3 = q_ref[h].reshape(nb, sub, dh)
            k3 = k_ref[h].reshape(nb, sub, dh)
            c3 = cum_ref[h].reshape(nb, sub, dh)
            last = c3[:, sub - 1:sub, :]
            k_side = (k3 * jnp.exp2(last - c3)).reshape(ts, dh).astype(BF16)
            q_side = []
            for delta in range(1, per):
                ref_c = jnp.concatenate([jnp.zeros((delta, 1, dh), F32), last[:nb - delta]], axis=0)
                expo = jnp.where(blk >= delta, c3 - ref_c, 0.0)
                q_side.append((q3 * jnp.exp2(expo)).reshape(ts, dh))
            cross = _dot_nt(jnp.concatenate(q_side, axis=0).astype(BF16), k_side)

            rows = slice((h - h0) * nb * half, (h - h0 + 1) * nb * half)
            within = jnp.concatenate([a_top[rows].reshape(nb, half, ts), a_bot[rows].reshape(nb, half, ts)],
                                     axis=1).reshape(ts, ts)
            scores = jnp.where(dist_is[0], within, 0.0)
            for delta in range(1, per):
                scores = jnp.where(dist_is[delta], cross[(delta - 1) * ts:delta * ts], scores)
            o = o_ref[h] + _dot(scores.astype(BF16), v_ref[h])
            act_ref[:, cols] = (_rms(o, hn_ref[:, cols]) * og_ref[h]).astype(BF16)

    y_ref[...] = x + _dot(act_ref[:, 0:d], wout_ref[...])


def _hgrn_mixer(h, norm_g, w_in, lb_logits, head_norm, w_out, layer, slot):
    b, s, d = h.shape
    ts, dh = HGRN_TILE, HGRN_HEAD_DIM
    assert s % ts == 0 and d % dh == 0 and ts % HGRN_CHUNK == 0 and HGRN_CHUNK % HGRN_SUB == 0
    heads = d // dh
    n_slots = lb_logits.shape[0]
    sel, dist = _hgrn_tables()
    prow = heads * (ts // HGRN_SUB) * (HGRN_SUB // 2)
    tile_spec = pl.BlockSpec((None, ts, d), lambda bi, si: (bi, si, 0))
    return pl.pallas_call(
        functools.partial(_hgrn_kernel, layer_slot=slot),
        grid=(b, s // ts),
        in_specs=[tile_spec, _layer_spec((1, d), layer), _layer_spec((d, 4 * d), slot),
                  _const_spec((n_slots, d)), _layer_spec((1, d), slot), _layer_spec((d, d), slot),
                  _const_spec(sel.shape), _const_spec(dist.shape)],
        out_specs=tile_spec,
        out_shape=jax.ShapeDtypeStruct((b, s, d), F32),
        scratch_shapes=[pltpu.VMEM((heads, ts, dh), F32), pltpu.VMEM((heads, ts, dh), F32),
                        pltpu.VMEM((heads, ts, dh), BF16), pltpu.VMEM((heads, ts, dh), F32),
                        pltpu.VMEM((heads, ts, dh), F32),
                        pltpu.VMEM((heads, ts, dh), F32), pltpu.VMEM((heads, ts, dh), F32),
                        pltpu.VMEM((prow, (HGRN_SUB // 2) * dh), BF16), pltpu.VMEM((prow, HGRN_SUB * dh), BF16),
                        pltpu.VMEM((ts, d + LANES), BF16), pltpu.VMEM((heads, dh, dh), F32)],
        compiler_params=_params("parallel", "arbitrary"),
        name="hgrn_mixer",
    )(h, norm_g, w_in, lb_logits, head_norm, w_out, sel, dist)


def _conv_shift_matrix():
    shift = np.zeros((CONV_ROWS, CONV_SHIFT_K), np.float32)
    t = np.arange(CONV_ROWS)
    for c in range(1, SUBLANES):
        shift[t, (c - 1) * CONV_PHASE_ROWS + CONV_UNIT + t - c] = 1.0
    return jnp.asarray(shift, BF16)


def _conv_kernel(x_ref, g_ref, win_ref, bin_ref, dwx_ref, dwb_ref, lng_ref, lnb_ref, wout_ref, bout_ref,
                 shift_ref, y_ref, u_ref, c_ref, v_ref):
    ts, d = x_ref.shape
    halo, sl, unit, depth = CONV_HALO, SUBLANES, CONV_UNIT, CONV_TAP_TILES

    @pl.when(pl.program_id(1) == 0)
    def _():
        u_ref[0:halo, 0:d] = jnp.zeros((halo, d), F32)
        pad = CONV_SHIFT_K - (sl - 1) * CONV_PHASE_ROWS
        v_ref[CONV_SHIFT_K - pad:CONV_SHIFT_K, 0:d] = jnp.zeros((pad, d), BF16)

    x = x_ref[...]
    xn = _rms(x, g_ref[...]).astype(BF16)
    chunk = CONV_COL_CHUNK
    for c0 in range(0, d, chunk):
        cc_ = slice(c0, c0 + chunk)
        a = _dot(xn, win_ref[:, c0:c0 + chunk]) + bin_ref[:, c0:c0 + chunk]
        gate = _dot(xn, win_ref[:, d + c0:d + c0 + chunk]) + bin_ref[:, d + c0:d + c0 + chunk]
        u_ref[halo:halo + ts, cc_] = a * jax.nn.sigmoid(gate)
        for col in range(c0 // LANES, (c0 + chunk) // LANES):
            cs = slice(col * LANES, (col + 1) * LANES)
            bias = jnp.broadcast_to(dwb_ref[:, cs], (sl, LANES))
            for m in range(-1, ts // unit):
                base = halo + m * unit - (depth - 1) * sl
                window = [u_ref[base + j * sl:base + (j + 1) * sl, cs] for j in range(depth + 1)]
                for c in range(sl):
                    if m < 0 and c == 0:
                        continue
                    top, bot = (bias, bias) if c == 0 else (None, None)
                    for a_i in range(depth):
                        r = a_i * sl + c
                        if r >= CONV_WIDTH:
                            continue
                        j = CONV_WIDTH - 1 - r
                        tap = dwx_ref[j * sl:(j + 1) * sl, cs]
                        t_term = tap * window[depth - 1 - a_i]
                        b_term = tap * window[depth - a_i]
                        top = t_term if top is None else top + t_term
                        bot = b_term if bot is None else bot + b_term
                    both = jnp.concatenate([top, bot], axis=0)
                    if c == 0:
                        c_ref[m * unit:(m + 1) * unit, cs] = both
                    else:
                        row0 = (c - 1) * CONV_PHASE_ROWS + (m + 1) * unit
                        v_ref[row0:row0 + unit, cs] = both.astype(BF16)
        u_ref[0:halo, cc_] = u_ref[ts:ts + halo, cc_]
        c_ref[:, cc_] = c_ref[:, cc_] + _dot(shift_ref[...], v_ref[:, cc_])

    c = c_ref[:, 0:d]
    mu = jnp.mean(c, axis=-1, keepdims=True)
    cc = c - mu
    var = jnp.mean(cc * cc, axis=-1, keepdims=True)
    z = cc * lax.rsqrt(var + LN_EPS) * lng_ref[...] + lnb_ref[...]
    z = (z * jax.nn.sigmoid(z)).astype(BF16)
    y_ref[...] = x + _dot(z, wout_ref[...]) + bout_ref[...]


def _conv_mixer(h, norm_g, w_in, b_in, dw, dw_b, ln_g, ln_b, w_out, b_out, layer, slot):
    b, s, d = h.shape
    ts = CONV_ROWS
    assert s % ts == 0 and ts % CONV_UNIT == 0 and ts >= CONV_HALO
    dwx = jnp.pad(jnp.repeat(dw[slot], SUBLANES, axis=0), ((0, 0), (0, CONV_PAD_LANES)))
    pitch = d + CONV_PAD_LANES
    shift = _conv_shift_matrix()
    tile_spec = pl.BlockSpec((None, ts, d), lambda bi, si: (bi, si, 0))
    vec = lambda: _layer_spec((1, d), slot)
    return pl.pallas_call(
        _conv_kernel,
        grid=(b, s // ts),
        in_specs=[tile_spec, _layer_spec((1, d), layer), _layer_spec((d, 2 * d), slot),
                  _layer_spec((1, 2 * d), slot), _const_spec((CONV_WIDTH * SUBLANES, pitch)), vec(), vec(),
                  vec(), _layer_spec((d, d), slot), vec(), _const_spec(shift.shape)],
        out_specs=tile_spec,
        out_shape=jax.ShapeDtypeStruct((b, s, d), F32),
        scratch_shapes=[pltpu.VMEM((ts + CONV_HALO, pitch), F32), pltpu.VMEM((ts, pitch), F32),
                        pltpu.VMEM((CONV_SHIFT_K, pitch), BF16)],
        compiler_params=_params("parallel", "arbitrary"),
        name="conv_mixer",
    )(h, norm_g, w_in, b_in, dwx, dw_b, ln_g, ln_b, w_out, b_out, shift)


def _memkv_kernel(m_ref, g_ref, w_ref, kv_ref):
    mn = _rms(m_ref[...], g_ref[...]).astype(BF16)
    kv_ref[...] = _dot(mn, w_ref[...]).astype(BF16)


def _memkv(mem2d, mem_norm, wkv):
    n, d = mem2d.shape
    n_layers = wkv.shape[0]
    rows = min(ROW_TILE, n)
    return pl.pallas_call(
        _memkv_kernel,
        grid=(n_layers, n // rows),
        in_specs=[pl.BlockSpec((rows, d), lambda li, i: (i, 0)), _const_spec((1, d)),
                  pl.BlockSpec((None, d, 2 * d), lambda li, i: (li, 0, 0))],
        out_specs=pl.BlockSpec((None, rows, 2 * d), lambda li, i: (li, i, 0)),
        out_shape=jax.ShapeDtypeStruct((n_layers, n, 2 * d), BF16),
        compiler_params=_params("arbitrary", "arbitrary"),
        name="mem_kv",
    )(mem2d, mem_norm.reshape(1, d), wkv)


def _xattn_kernel(x_ref, g_ref, wq_ref, kv_ref, wo_ref, y_ref, act_ref, *, heads):
    ts, d = x_ref.shape
    dh = d // heads
    scale = dh ** -0.5
    x = x_ref[...]
    xn = _rms(x, g_ref[...]).astype(BF16)
    q = _dot(xn, wq_ref[...]).astype(BF16)
    for h in range(heads):
        cols = slice(h * dh, (h + 1) * dh)
        s = _dot_nt(q[:, cols], kv_ref[:, cols]) * scale
        s = s - jnp.max(s, axis=-1, keepdims=True)
        p = jnp.exp(s)
        p = p / jnp.sum(p, axis=-1, keepdims=True)
        act_ref[:, cols] = _dot(p.astype(BF16), kv_ref[:, d + h * dh:d + (h + 1) * dh]).astype(BF16)
    y_ref[...] = x + _dot(act_ref[...], wo_ref[...])


def _xattn(h, norm_g, wq, kv, wo, layer, heads):
    b, s, d = h.shape
    m = kv.shape[2]
    ts = XATTN_ROWS
    assert s % ts == 0 and d % heads == 0
    tile_spec = pl.BlockSpec((None, ts, d), lambda bi, si: (bi, si, 0))
    return pl.pallas_call(
        functools.partial(_xattn_kernel, heads=heads),
        grid=(b, s // ts),
        in_specs=[tile_spec, _layer_spec((1, d), layer), _layer_spec((d, d), layer),
                  pl.BlockSpec((None, None, m, 2 * d), lambda bi, si: (layer, bi, 0, 0)),
                  _layer_spec((d, d), layer)],
        out_specs=tile_spec,
        out_shape=jax.ShapeDtypeStruct((b, s, d), F32),
        scratch_shapes=[pltpu.VMEM((ts, d), BF16)],
        compiler_params=_params("parallel", "parallel"),
        name="xattn",
    )(h, norm_g, wq, kv, wo)


def kernel(x, mem, ffn1_norm, ffn1_w_in, ffn1_w_out, mix_norm, hgrn_w_in, hgrn_head_norm, hgrn_w_out,
           hgrn_lb_logits, conv_w_in, conv_b_in, conv_dw, conv_dw_b, conv_ln_g, conv_ln_b, conv_w_out,
           conv_b_out, xattn_norm, xattn_wq, xattn_wkv, xattn_wo, ffn2_norm, ffn2_w_in, ffn2_w_out,
           mem_norm, final_norm):
    b, s, d = x.shape
    m = mem.shape[1]
    depth = ffn1_norm.shape[0]
    t = b * s
    bf = lambda w: w.astype(BF16)
    vec = lambda a: a.reshape(a.shape[0], 1, a.shape[1])
    flat = lambda a: a.reshape(t, d)
    cube = lambda a: a.reshape(b, s, d)

    ffn1_w_in, ffn1_w_out, ffn2_w_in, ffn2_w_out = bf(ffn1_w_in), bf(ffn1_w_out), bf(ffn2_w_in), bf(ffn2_w_out)
    hgrn_w_in, hgrn_w_out = bf(hgrn_w_in), bf(hgrn_w_out)
    conv_w_in, conv_w_out = bf(conv_w_in), bf(conv_w_out)
    xattn_wq, xattn_wo = bf(xattn_wq), bf(xattn_wo)
    ffn1_norm, ffn2_norm, mix_norm, xattn_norm = vec(ffn1_norm), vec(ffn2_norm), vec(mix_norm), vec(xattn_norm)

    kv = _memkv(mem.reshape(b * m, d), mem_norm, bf(xattn_wkv)).reshape(depth, b, m, 2 * d)
    h = x
    for layer in range(depth):
        h = cube(_ffn(flat(h), ffn1_norm, ffn1_w_in, ffn1_w_out, layer))
        j = layer // 2
        if layer % 2 == 0:
            h = _hgrn_mixer(h, mix_norm, hgrn_w_in, hgrn_lb_logits, vec(hgrn_head_norm), hgrn_w_out, layer, j)
        else:
            h = _conv_mixer(h, mix_norm, conv_w_in, vec(conv_b_in), conv_dw, vec(conv_dw_b), vec(conv_ln_g),
                            vec(conv_ln_b), conv_w_out, vec(conv_b_out), layer, j)
        h = _xattn(h, xattn_norm, xattn_wq, kv, xattn_wo, layer, XATTN_HEADS)
        last = layer == depth - 1
        h = cube(_ffn(flat(h), ffn2_norm, ffn2_w_in, ffn2_w_out, layer, final_norm if last else None))
    return h
```

```python
import functools

import numpy as np
import jax
import jax.numpy as jnp
from jax import lax
from jax.experimental import pallas as pl
from jax.experimental.pallas import tpu as pltpu

F32 = jnp.float32
BF16 = jnp.bfloat16

RMS_EPS = 1e-6
LN_EPS = 1e-5

HGRN_HEAD_DIM = 128
XATTN_HEADS = 4
CONV_WIDTH = 31

LANES = 128
SUBLANES = 8
VMEM_LIMIT_BYTES = 56 * 1024 * 1024

HGRN_TILE = 2 * LANES
HGRN_CHUNK = 64
HGRN_SUB = 2 * SUBLANES

FFN_ROWS = 1024
FFN_COLS = 256
ROW_TILE = 512
XATTN_ROWS = 1024
CONV_ROWS = 256
CONV_UNIT = 2 * SUBLANES
CONV_TAP_TILES = -(-CONV_WIDTH // SUBLANES)
CONV_HALO = CONV_UNIT + (CONV_TAP_TILES - 1) * SUBLANES
CONV_PHASE_ROWS = CONV_ROWS + CONV_UNIT
CONV_SHIFT_K = -(-((SUBLANES - 1) * CONV_PHASE_ROWS) // LANES) * LANES
CONV_PAD_LANES = LANES
CONV_COL_CHUNK = 2 * LANES


def _const_spec(shape):
    zeros = (0,) * len(shape)
    return pl.BlockSpec(shape, lambda *_: zeros, pipeline_mode=pl.Buffered(1))


def _layer_spec(shape, layer):
    zeros = (0,) * len(shape)
    return pl.BlockSpec((None,) + tuple(shape), lambda *_: (layer,) + zeros,
                        pipeline_mode=pl.Buffered(1))


def _params(*semantics):
    return pltpu.CompilerParams(dimension_semantics=semantics,
                                vmem_limit_bytes=VMEM_LIMIT_BYTES)


def _rms(x, g):
    ms = jnp.mean(x * x, axis=-1, keepdims=True)
    return x * lax.rsqrt(ms + RMS_EPS) * g


def _dot(a, b):
    return jnp.dot(a, b, preferred_element_type=F32)


def _dot_nt(a, b):
    return lax.dot_general(a, b, (((1,), (1,)), ((), ())), preferred_element_type=F32)


def _dot_tn(a, b):
    return lax.dot_general(a, b, (((0,), (0,)), ((), ())), preferred_element_type=F32)


def _ffn_kernel(*refs, d_ff, final):
    if final:
        x_ref, g_ref, win_ref, wout_ref, fg_ref, o_ref, act_ref = refs
    else:
        x_ref, g_ref, win_ref, wout_ref, o_ref, act_ref = refs
    x = x_ref[...]
    xn = _rms(x, g_ref[...]).astype(BF16)
    for c in range(d_ff // FFN_COLS):
        lo = c * FFN_COLS
        gate = _dot(xn, win_ref[:, lo:lo + FFN_COLS])
        up = _dot(xn, win_ref[:, d_ff + lo:d_ff + lo + FFN_COLS])
        act_ref[:, lo:lo + FFN_COLS] = (gate * jax.nn.sigmoid(gate) * up).astype(BF16)
    y = x + 0.5 * _dot(act_ref[...], wout_ref[...])
    if final:
        y = _rms(y, fg_ref[...])
    o_ref[...] = y


def _ffn(h, norm_g, w_in, w_out, layer, final_g=None):
    t, d = h.shape
    d_ff = w_out.shape[1]
    assert t % FFN_ROWS == 0 and d_ff % FFN_COLS == 0
    final = final_g is not None
    row_spec = pl.BlockSpec((FFN_ROWS, d), lambda i: (i, 0))
    in_specs = [row_spec, _layer_spec((1, d), layer), _layer_spec((d, 2 * d_ff), layer),
                _layer_spec((d_ff, d), layer)]
    args = [h, norm_g, w_in, w_out]
    if final:
        in_specs.append(_const_spec((1, d)))
        args.append(final_g.reshape(1, d))
    return pl.pallas_call(
        functools.partial(_ffn_kernel, d_ff=d_ff, final=final),
        grid=(t // FFN_ROWS,),
        in_specs=in_specs,
        out_specs=row_spec,
        out_shape=jax.ShapeDtypeStruct((t, d), F32),
        scratch_shapes=[pltpu.VMEM((FFN_ROWS, d_ff), BF16)],
        compiler_params=_params("parallel"),
        name="ffn",
    )(*args)


def _hgrn_tables():
    sel = np.zeros((HGRN_SUB * LANES, HGRN_TILE), np.float32)
    for s in range(HGRN_SUB):
        sel[s * LANES:(s + 1) * LANES, s::HGRN_SUB] = 1.0
    r = np.arange(HGRN_TILE)
    same_chunk = (r[:, None] // HGRN_CHUNK) == (r[None, :] // HGRN_CHUNK)
    dist = r[:, None] // HGRN_SUB - r[None, :] // HGRN_SUB
    dist = np.where(same_chunk & (dist >= 0), dist, -1).astype(np.int32)
    return jnp.asarray(sel, BF16), jnp.asarray(dist)


def _hgrn_kernel(x_ref, g_ref, win_ref, lbl_ref, hn_ref, wout_ref, sel_ref, dist_ref, y_ref,
                 q_ref, k_ref, v_ref, cum_ref, kb_ref, og_ref, o_ref, ptop_ref, pbot_ref, act_ref, st_ref,
                 *, layer_slot):
    ts, d = x_ref.shape
    dh, sub, chunk = HGRN_HEAD_DIM, HGRN_SUB, HGRN_CHUNK
    heads, nb, nc, per = d // dh, ts // sub, ts // chunk, chunk // sub
    half = sub // 2
    head_cols = [slice(h * dh, (h + 1) * dh) for h in range(heads)]

    @pl.when(pl.program_id(1) == 0)
    def _():
        st_ref[...] = jnp.zeros_like(st_ref)

    x = x_ref[...]
    xn = _rms(x, g_ref[...]).astype(BF16)
    logits = lbl_ref[...]
    e = jnp.exp(logits - jnp.max(logits, axis=0, keepdims=True))
    sm = e / jnp.sum(e, axis=0, keepdims=True)
    lb = jnp.sum(sm[0:layer_slot + 1], axis=0, keepdims=True)

    qp = _dot(xn, win_ref[:, 0:d])
    qs = qp * jax.nn.sigmoid(qp)
    fp = _dot(xn, win_ref[:, d:2 * d])
    ks = (1.0 - lb) * jax.nn.sigmoid(-fp)
    cum = jnp.log2(lb + (1.0 - lb) * jax.nn.sigmoid(fp))
    row = lax.broadcasted_iota(jnp.int32, (ts, d), 0) % chunk
    sh = 1
    while sh < chunk:
        cum = cum + jnp.where(row >= sh, pltpu.roll(cum, sh, axis=0), 0.0)
        sh *= 2
    kb = cum - jnp.where(ks > 0.0, jnp.log2(ks), -jnp.inf)
    for h, cols in enumerate(head_cols):
        q_ref[h] = qs[:, cols]
        k_ref[h] = ks[:, cols]
        cum_ref[h] = cum[:, cols]
        kb_ref[h] = kb[:, cols]
    vp = _dot(xn, win_ref[:, 2 * d:3 * d]).astype(BF16)
    gp = _dot(xn, win_ref[:, 3 * d:4 * d])
    gs = gp * jax.nn.sigmoid(gp)
    for h, cols in enumerate(head_cols):
        v_ref[h] = vp[:, cols]
        og_ref[h] = gs[:, cols]

    q_dec, k_dec, c_end = [], [], []
    for h in range(heads):
        cum_h = cum_ref[h]
        c4 = cum_h.reshape(nc, chunk, dh)
        end = c4[:, chunk - 1:chunk, :]
        c_end.append(end)
        q_dec.append((q_ref[h] * jnp.exp2(cum_h)).astype(BF16))
        k_dec.append((k_ref[h].reshape(nc, chunk, dh) * jnp.exp2(end - c4)).astype(BF16))
    states = [st_ref[h] for h in range(heads)]
    for c in range(nc):
        crow = slice(c * chunk, (c + 1) * chunk)
        for h, cols in enumerate(head_cols):
            o_ref[h, crow, :] = _dot_nt(q_dec[h][crow], states[h].astype(BF16))
            states[h] = states[h] * jnp.exp2(c_end[h][c]) + _dot_tn(v_ref[h, crow, :], k_dec[h][c])
    for h in range(heads):
        st_ref[h] = states[h]

    dist = dist_ref[...]
    dist_is = [dist == delta for delta in range(per)]
    blk = lax.broadcasted_iota(jnp.int32, (nb, 1, dh), 0) % per
    sub_iota = lax.broadcasted_iota(jnp.int32, (half, dh), 0)
    group = 2
    for h0 in range(0, heads, group):
        for h in range(h0, h0 + group):
            cols = head_cols[h]
            for b0 in range(0, nb, 2):
                tops, bots = [], []
                for r0 in (b0 * sub, (b0 + 1) * sub):
                    q_top, q_bot = q_ref[h, r0:r0 + half, :], q_ref[h, r0 + half:r0 + sub, :]
                    c_top, c_bot = cum_ref[h, r0:r0 + half, :], cum_ref[h, r0 + half:r0 + sub, :]
                    p_top, p_bot = [], []
                    for s in range(sub):
                        b_s = kb_ref[h, r0 + s:r0 + s + 1, :]
                        if s < half:
                            d_top = c_top - b_s
                            if s > 0:
                                d_top = jnp.where(sub_iota >= s, d_top, -jnp.inf)
                            p_top.append(jnp.exp2(d_top) * q_top)
                            p_bot.append(jnp.exp2(c_bot - b_s) * q_bot)
                        else:
                            d_bot = c_bot - b_s
                            if s > half:
                                d_bot = jnp.where(sub_iota >= s - half, d_bot, -jnp.inf)
                            p_bot.append(jnp.exp2(d_bot) * q_bot)
                    tops.append(jnp.concatenate(p_top, axis=-1))
                    bots.append(jnp.concatenate(p_bot, axis=-1))
                prow = (h * nb + b0) * half
                ptop_ref[prow:prow + 2 * half, :] = jnp.concatenate(tops, axis=0).astype(BF16)
                pbot_ref[prow:prow + 2 * half, :] = jnp.concatenate(bots, axis=0).astype(BF16)
        grows = slice(h0 * nb * half, (h0 + group) * nb * half)
        a_top = _dot(ptop_ref[grows, :], sel_ref[0:half * dh, :])
        a_bot = _dot(pbot_ref[grows, :], sel_ref[...])

        for h in range(h0, h0 + group):
            cols = head_cols[h]
            q3 = q_ref[h].reshape(nb, sub, dh)
            k3 = k_ref[h].reshape(nb, sub, dh)
            c3 = cum_ref[h].reshape(nb, sub, dh)
            last = c3[:, sub - 1:sub, :]
            k_side = (k3 * jnp.exp2(last - c3)).reshape(ts, dh).astype(BF16)
            q_side = []
            for delta in range(1, per):
                ref_c = jnp.concatenate([jnp.zeros((delta, 1, dh), F32), last[:nb - delta]], axis=0)
                expo = jnp.where(blk >= delta, c3 - ref_c, 0.0)
                q_side.append((q3 * jnp.exp2(expo)).reshape(ts, dh))
            cross = _dot_nt(jnp.concatenate(q_side, axis=0).astype(BF16), k_side)

            rows = slice((h - h0) * nb * half, (h - h0 + 1) * nb * half)
            within = jnp.concatenate([a_top[rows].reshape(nb, half, ts), a_bot[rows].reshape(nb, half, ts)],
                                     axis=1).reshape(ts, ts)
            scores = jnp.where(dist_is[0], within, 0.0)
            for delta in range(1, per):
                scores = jnp.where(dist_is[delta], cross[(delta - 1) * ts:delta * ts], scores)
            o = o_ref[h] + _dot(scores.astype(BF16), v_ref[h])
            act_ref[:, cols] = (_rms(o, hn_ref[:, cols]) * og_ref[h]).astype(BF16)

    y_ref[...] = x + _dot(act_ref[:, 0:d], wout_ref[...])


def _hgrn_mixer(h, norm_g, w_in, lb_logits, head_norm, w_out, layer, slot):
    b, s, d = h.shape
    ts, dh = HGRN_TILE, HGRN_HEAD_DIM
    assert s % ts == 0 and d % dh == 0 and ts % HGRN_CHUNK == 0 and HGRN_CHUNK % HGRN_SUB == 0
    heads = d // dh
    n_slots = lb_logits.shape[0]
    sel, dist = _hgrn_tables()
    prow = heads * (ts // HGRN_SUB) * (HGRN_SUB // 2)
    tile_spec = pl.BlockSpec((None, ts, d), lambda bi, si: (bi, si, 0))
    return pl.pallas_call(
        functools.partial(_hgrn_kernel, layer_slot=slot),
        grid=(b, s // ts),
        in_specs=[tile_spec, _layer_spec((1, d), layer), _layer_spec((d, 4 * d), slot),
                  _const_spec((n_slots, d)), _layer_spec((1, d), slot), _layer_spec((d, d), slot),
                  _const_spec(sel.shape), _const_spec(dist.shape)],
        out_specs=tile_spec,
        out_shape=jax.ShapeDtypeStruct((b, s, d), F32),
        scratch_shapes=[pltpu.VMEM((heads, ts, dh), F32), pltpu.VMEM((heads, ts, dh), F32),
                        pltpu.VMEM((heads, ts, dh), BF16), pltpu.VMEM((heads, ts, dh), F32),
                        pltpu.VMEM((heads, ts, dh), F32),
                        pltpu.VMEM((heads, ts, dh), F32), pltpu.VMEM((heads, ts, dh), F32),
                        pltpu.VMEM((prow, (HGRN_SUB // 2) * dh), BF16), pltpu.VMEM((prow, HGRN_SUB * dh), BF16),
                        pltpu.VMEM((ts, d + LANES), BF16), pltpu.VMEM((heads, dh, dh), F32)],
        compiler_params=_params("parallel", "arbitrary"),
        name="hgrn_mixer",
    )(h, norm_g, w_in, lb_logits, head_norm, w_out, sel, dist)


def _conv_shift_matrix():
    shift = np.zeros((CONV_ROWS, CONV_SHIFT_K), np.float32)
    t = np.arange(CONV_ROWS)
    for c in range(1, SUBLANES):
        shift[t, (c - 1) * CONV_PHASE_ROWS + CONV_UNIT + t - c] = 1.0
    return jnp.asarray(shift, BF16)


def _conv_kernel(x_ref, g_ref, win_ref, bin_ref, dwx_ref, dwb_ref, lng_ref, lnb_ref, wout_ref, bout_ref,
                 shift_ref, y_ref, u_ref, c_ref, v_ref):
    ts, d = x_ref.shape
    halo, sl, unit, depth = CONV_HALO, SUBLANES, CONV_UNIT, CONV_TAP_TILES

    @pl.when(pl.program_id(1) == 0)
    def _():
        u_ref[0:halo, 0:d] = jnp.zeros((halo, d), F32)
        pad = CONV_SHIFT_K - (sl - 1) * CONV_PHASE_ROWS
        v_ref[CONV_SHIFT_K - pad:CONV_SHIFT_K, 0:d] = jnp.zeros((pad, d), BF16)

    x = x_ref[...]
    xn = _rms(x, g_ref[...]).astype(BF16)
    chunk = CONV_COL_CHUNK
    for c0 in range(0, d, chunk):
        cc_ = slice(c0, c0 + chunk)
        a = _dot(xn, win_ref[:, c0:c0 + chunk]) + bin_ref[:, c0:c0 + chunk]
        gate = _dot(xn, win_ref[:, d + c0:d + c0 + chunk]) + bin_ref[:, d + c0:d + c0 + chunk]
        u_ref[halo:halo + ts, cc_] = a * jax.nn.sigmoid(gate)
        for col in range(c0 // LANES, (c0 + chunk) // LANES):
            cs = slice(col * LANES, (col + 1) * LANES)
            bias = jnp.broadcast_to(dwb_ref[:, cs], (sl, LANES))
            for m in range(-1, ts // unit):
                base = halo + m * unit - (depth - 1) * sl
                window = [u_ref[base + j * sl:base + (j + 1) * sl, cs] for j in range(depth + 1)]
                for c in range(sl):
                    if m < 0 and c == 0:
                        continue
                    top, bot = (bias, bias) if c == 0 else (None, None)
                    for a_i in range(depth):
                        r = a_i * sl + c
                        if r >= CONV_WIDTH:
                            continue
                        j = CONV_WIDTH - 1 - r
                        tap = dwx_ref[j * sl:(j + 1) * sl, cs]
                        t_term = tap * window[depth - 1 - a_i]
                        b_term = tap * window[depth - a_i]
                        top = t_term if top is None else top + t_term
                        bot = b_term if bot is None else bot + b_term
                    both = jnp.concatenate([top, bot], axis=0)
                    if c == 0:
                        c_ref[m * unit:(m + 1) * unit, cs] = both
                    else:
                        row0 = (c - 1) * CONV_PHASE_ROWS + (m + 1) * unit
                        v_ref[row0:row0 + unit, cs] = both.astype(BF16)
        u_ref[0:halo, cc_] = u_ref[ts:ts + halo, cc_]
        c_ref[:, cc_] = c_ref[:, cc_] + _dot(shift_ref[...], v_ref[:, cc_])

    c = c_ref[:, 0:d]
    mu = jnp.mean(c, axis=-1, keepdims=True)
    cc = c - mu
    var = jnp.mean(cc * cc, axis=-1, keepdims=True)
    z = cc * lax.rsqrt(var + LN_EPS) * lng_ref[...] + lnb_ref[...]
    z = (z * jax.nn.sigmoid(z)).astype(BF16)
    y_ref[...] = x + _dot(z, wout_ref[...]) + bout_ref[...]


def _conv_mixer(h, norm_g, w_in, b_in, dw, dw_b, ln_g, ln_b, w_out, b_out, layer, slot):
    b, s, d = h.shape
    ts = CONV_ROWS
    assert s % ts == 0 and ts % CONV_UNIT == 0 and ts >= CONV_HALO
    dwx = jnp.pad(jnp.repeat(dw[slot], SUBLANES, axis=0), ((0, 0), (0, CONV_PAD_LANES)))
    pitch = d + CONV_PAD_LANES
    shift = _conv_shift_matrix()
    tile_spec = pl.BlockSpec((None, ts, d), lambda bi, si: (bi, si, 0))
    vec = lambda: _layer_spec((1, d), slot)
    return pl.pallas_call(
        _conv_kernel,
        grid=(b, s // ts),
        in_specs=[tile_spec, _layer_spec((1, d), layer), _layer_spec((d, 2 * d), slot),
                  _layer_spec((1, 2 * d), slot), _const_spec((CONV_WIDTH * SUBLANES, pitch)), vec(), vec(),
                  vec(), _layer_spec((d, d), slot), vec(), _const_spec(shift.shape)],
        out_specs=tile_spec,
        out_shape=jax.ShapeDtypeStruct((b, s, d), F32),
        scratch_shapes=[pltpu.VMEM((ts + CONV_HALO, pitch), F32), pltpu.VMEM((ts, pitch), F32),
                        pltpu.VMEM((CONV_SHIFT_K, pitch), BF16)],
        compiler_params=_params("parallel", "arbitrary"),
        name="conv_mixer",
    )(h, norm_g, w_in, b_in, dwx, dw_b, ln_g, ln_b, w_out, b_out, shift)


def _xattn_kernel(x_ref, g_ref, wq_ref, mem_ref, mg_ref, wkv_ref, wo_ref, y_ref, kv_ref, act_ref, *, heads):
    ts, d = x_ref.shape
    dh = d // heads
    scale = dh ** -0.5

    @pl.when(pl.program_id(1) == 0)
    def _():
        mn = _rms(mem_ref[...], mg_ref[...]).astype(BF16)
        kv_ref[...] = _dot(mn, wkv_ref[...]).astype(BF16)

    x = x_ref[...]
    xn = _rms(x, g_ref[...]).astype(BF16)
    q = _dot(xn, wq_ref[...]).astype(BF16)
    for h in range(heads):
        cols = slice(h * dh, (h + 1) * dh)
        s = _dot_nt(q[:, cols], kv_ref[:, cols]) * scale
        s = s - jnp.max(s, axis=-1, keepdims=True)
        p = jnp.exp(s)
        p = p / jnp.sum(p, axis=-1, keepdims=True)
        act_ref[:, cols] = _dot(p.astype(BF16), kv_ref[:, d + h * dh:d + (h + 1) * dh]).astype(BF16)
    y_ref[...] = x + _dot(act_ref[...], wo_ref[...])


def _xattn(h, norm_g, wq, mem, mem_norm, wkv, wo, layer, heads):
    b, s, d = h.shape
    m = mem.shape[1]
    ts = XATTN_ROWS
    assert s % ts == 0 and d % heads == 0
    tile_spec = pl.BlockSpec((None, ts, d), lambda bi, si: (bi, si, 0))
    return pl.pallas_call(
        functools.partial(_xattn_kernel, heads=heads),
        grid=(b, s // ts),
        in_specs=[tile_spec, _layer_spec((1, d), layer), _layer_spec((d, d), layer),
                  pl.BlockSpec((None, m, d), lambda bi, si: (bi, 0, 0)), _const_spec((1, d)),
                  _layer_spec((d, 2 * d), layer), _layer_spec((d, d), layer)],
        out_specs=tile_spec,
        out_shape=jax.ShapeDtypeStruct((b, s, d), F32),
        scratch_shapes=[pltpu.VMEM((m, 2 * d), BF16), pltpu.VMEM((ts, d), BF16)],
        compiler_params=_params("parallel", "arbitrary"),
        name="xattn",
    )(h, norm_g, wq, mem, mem_norm.reshape(1, d), wkv, wo)


def kernel(x, mem, ffn1_norm, ffn1_w_in, ffn1_w_out, mix_norm, hgrn_w_in, hgrn_head_norm, hgrn_w_out,
           hgrn_lb_logits, conv_w_in, conv_b_in, conv_dw, conv_dw_b, conv_ln_g, conv_ln_b, conv_w_out,
           conv_b_out, xattn_norm, xattn_wq, xattn_wkv, xattn_wo, ffn2_norm, ffn2_w_in, ffn2_w_out,
           mem_norm, final_norm):
    b, s, d = x.shape
    m = mem.shape[1]
    depth = ffn1_norm.shape[0]
    t = b * s
    bf = lambda w: w.astype(BF16)
    vec = lambda a: a.reshape(a.shape[0], 1, a.shape[1])
    flat = lambda a: a.reshape(t, d)
    cube = lambda a: a.reshape(b, s, d)

    ffn1_w_in, ffn1_w_out, ffn2_w_in, ffn2_w_out = bf(ffn1_w_in), bf(ffn1_w_out), bf(ffn2_w_in), bf(ffn2_w_out)
    hgrn_w_in, hgrn_w_out = bf(hgrn_w_in), bf(hgrn_w_out)
    conv_w_in, conv_w_out = bf(conv_w_in), bf(conv_w_out)
    xattn_wq, xattn_wo = bf(xattn_wq), bf(xattn_wo)
    ffn1_norm, ffn2_norm, mix_norm, xattn_norm = vec(ffn1_norm), vec(ffn2_norm), vec(mix_norm), vec(xattn_norm)

    xattn_wkv = bf(xattn_wkv)
    h = x
    for layer in range(depth):
        h = cube(_ffn(flat(h), ffn1_norm, ffn1_w_in, ffn1_w_out, layer))
        j = layer // 2
        if layer % 2 == 0:
            h = _hgrn_mixer(h, mix_norm, hgrn_w_in, hgrn_lb_logits, vec(hgrn_head_norm), hgrn_w_out, layer, j)
        else:
            h = _conv_mixer(h, mix_norm, conv_w_in, vec(conv_b_in), conv_dw, vec(conv_dw_b), vec(conv_ln_g),
                            vec(conv_ln_b), conv_w_out, vec(conv_b_out), layer, j)
        h = _xattn(h, xattn_norm, xattn_wq, mem, mem_norm, xattn_wkv, xattn_wo, layer, XATTN_HEADS)
        last = layer == depth - 1
        h = cube(_ffn(flat(h), ffn2_norm, ffn2_w_in, ffn2_w_out, layer, final_norm if last else None))
    return h
```
